```python
import math
import jax, jax.numpy as jnp
from jax import lax
import numpy as np

D_MODEL = 4096
BATCH = 2
SEQ = 8192
DEPTH = 4

MEM_LEN = 256
N_EVEN = (DEPTH + 1) // 2
N_ODD = DEPTH // 2

DIFF_HEADS = 8
DIFF_HEAD_DIM = D_MODEL // 32
DIFF_V_DIM = 2 * DIFF_HEAD_DIM
DIFF_QK_WIDTH = DIFF_HEADS * 2 * DIFF_HEAD_DIM
DIFF_WIDTH = DIFF_HEADS * DIFF_V_DIM
ATTN_Q_BLOCK = 128
HGRN_HEADS = 16
HGRN_KEY_DIM = D_MODEL // 32
HGRN_VAL_DIM = D_MODEL // 32
HGRN_KEY_WIDTH = HGRN_HEADS * HGRN_KEY_DIM
HGRN_WIDTH = HGRN_HEADS * HGRN_VAL_DIM
HGRN_CHUNK = 64
LB_FLOOR = 1e-30
EVEN_SIZES = (DIFF_QK_WIDTH, DIFF_QK_WIDTH, DIFF_WIDTH, HGRN_KEY_WIDTH, HGRN_KEY_WIDTH, HGRN_WIDTH, HGRN_WIDTH)
EVEN_IN = sum(EVEN_SIZES)
EVEN_MIX = DIFF_WIDTH + HGRN_WIDTH
NSA_HEADS = 32
NSA_GROUPS = 4
NSA_REP = NSA_HEADS // NSA_GROUPS
NSA_HEAD_DIM = D_MODEL // NSA_HEADS
NSA_KV_WIDTH = NSA_GROUPS * NSA_HEAD_DIM
NSA_WIDTH = NSA_HEADS * NSA_HEAD_DIM
CMP_LEN = 32
CMP_STRIDE = 16
SLC_LEN = 64
SLC_TOPN = 16
WINDOW = 512
NSA_Q_BLOCK = 64
FORCE_SCORE = 1e9
ODD_SIZES = (NSA_WIDTH,) + (NSA_KV_WIDTH,) * 6 + (3 * NSA_HEADS,)
ODD_IN = sum(ODD_SIZES)
XATTN_HEADS = 4
XATTN_HEAD_DIM = 128
XATTN_WIDTH = XATTN_HEADS * XATTN_HEAD_DIM
PEER_HEADS = 8
PEER_N_KEYS = 128
PEER_N_EXPERTS = PEER_N_KEYS * PEER_N_KEYS
PEER_QUERY_DIM = 256
PEER_TOPK = 16
PEER_CHUNK = 128
DEEPNORM_ALPHA = (2.0 * DEPTH) ** 0.25
DEEPNORM_BETA = (8.0 * DEPTH) ** -0.25
LN_EPS = 1e-5
RMS_EPS = 1e-6
NEG_INF = -1e30

kernel_name = 'hybrid_diffattn_hgrn2_nsa_peer_deepnorm'


def layer_norm(x, g, b):
    xf = x.astype(jnp.float32)
    mu = jnp.mean(xf, axis=-1, keepdims=True)
    var = jnp.mean(jnp.square(xf - mu), axis=-1, keepdims=True)
    return ((xf - mu) * lax.rsqrt(var + LN_EPS)).astype(x.dtype) * g + b


def rms_norm(x, g):
    xf = x.astype(jnp.float32)
    y = xf * lax.rsqrt(jnp.mean(xf * xf, axis=-1, keepdims=True) + RMS_EPS)
    return y.astype(x.dtype) * g


def masked_softmax(s, mask):
    s = jnp.where(mask, s.astype(jnp.float32), NEG_INF)
    return jax.nn.softmax(s, axis=-1) * mask


def split_cols(z, sizes):
    out, start = [], 0
    for s in sizes:
        out.append(z[..., start:start + s])
        start += s
    return out


def diff_attention(q, k, v, lam, lam_init, norm_g):
    B, T, H = q.shape[:3]
    scale = DIFF_HEAD_DIM ** -0.5
    kpos = jnp.arange(T)

    def block(i):
        q0 = i * ATTN_Q_BLOCK
        qb = lax.dynamic_slice_in_dim(q, q0, ATTN_Q_BLOCK, axis=1)
        s = jnp.einsum('bqhcd,bkhcd->bhcqk', qb, k).astype(jnp.float32) * scale
        qpos = q0 + jnp.arange(ATTN_Q_BLOCK)
        p = masked_softmax(s, kpos[None, :] <= qpos[:, None])
        w = p[:, :, 0] - lam * p[:, :, 1]
        return jnp.einsum('bhqk,bkhd->bqhd', w.astype(v.dtype), v)

    o = lax.map(block, jnp.arange(T // ATTN_Q_BLOCK))
    o = jnp.moveaxis(o, 0, 1).reshape(B, T, H, DIFF_V_DIM)
    o = rms_norm(o, norm_g) * (1.0 - lam_init)
    return o.reshape(B, T, H * DIFF_V_DIM)


def hgrn2(q, log_f, k, i):
    B, T, H, dk = q.shape
    dv = i.shape[-1]
    C = HGRN_CHUNK
    n = T // C

    def to_chunks(a):
        return jnp.moveaxis(a.astype(jnp.float32).reshape(B, n, C, H, a.shape[-1]), 1, 0)

    causal = jnp.tril(jnp.ones((C, C), dtype=bool))[None, :, :, None, None]

    def step(S, xs):
        qc, lfc, kc, ic = xs
        b = jnp.cumsum(lfc, axis=1)
        o_inter = jnp.einsum('bthk,bhkv->bthv', qc * jnp.exp(b), S)
        rel = jnp.where(causal, b[:, :, None] - b[:, None, :], NEG_INF)
        A = jnp.einsum('bthk,btshk,bshk->bhts', qc, jnp.exp(rel), kc)
        o_intra = jnp.einsum('bhts,bshv->bthv', A, ic)
        b_last = b[:, -1]
        S = S * jnp.exp(b_last)[..., None] + jnp.einsum(
            'bshk,bshv->bhkv', kc * jnp.exp(b_last[:, None] - b), ic)
        return S, o_inter + o_intra

    S0 = jnp.zeros((B, H, dk, dv), jnp.float32)
    _, o = lax.scan(step, S0, (to_chunks(q), to_chunks(log_f), to_chunks(k), to_chunks(i)))
    return jnp.moveaxis(o, 0, 1).reshape(B, T, H, dv)


def even_mixer(x, w_in, w_out, lam_vec, diff_g, lb, hgrn_g, lam_init):
    B, T, _ = x.shape
    aq, ak, av, bq, bf, bi, bg = split_cols(x @ w_in, EVEN_SIZES)
    q = aq.reshape(B, T, DIFF_HEADS, 2, DIFF_HEAD_DIM)
    k = ak.reshape(B, T, DIFF_HEADS, 2, DIFF_HEAD_DIM)
    v = av.reshape(B, T, DIFF_HEADS, DIFF_V_DIM)
    lv = lam_vec.astype(jnp.float32)
    lam = jnp.exp(jnp.sum(lv[0] * lv[1])) - jnp.exp(jnp.sum(lv[2] * lv[3])) + lam_init
    o_a = diff_attention(q, k, v, lam, lam_init, diff_g)
    zf = bf.astype(jnp.float32).reshape(B, T, HGRN_HEADS, HGRN_KEY_DIM)
    lbh = lb.reshape(HGRN_HEADS, HGRN_KEY_DIM)
    log_lb = jnp.log(jnp.maximum(lbh, LB_FLOOR))
    log_f = jnp.logaddexp(log_lb, jnp.log1p(-lbh) + jax.nn.log_sigmoid(zf))
    k_in = (1.0 - lbh) * jax.nn.sigmoid(-zf)
    qh = jax.nn.silu(bq).reshape(B, T, HGRN_HEADS, HGRN_KEY_DIM)
    ih = bi.reshape(B, T, HGRN_HEADS, HGRN_VAL_DIM)
    o_b = hgrn2(qh, log_f, k_in, ih)
    gate = jax.nn.silu(bg.astype(jnp.float32)).reshape(B, T, HGRN_HEADS, HGRN_VAL_DIM)
    o_b = (rms_norm(o_b, hgrn_g) * gate).astype(x.dtype).reshape(B, T, HGRN_WIDTH)
    return jnp.concatenate([o_a.astype(x.dtype), o_b], axis=-1) @ w_out


def nsa(q, k_c, v_c, k_s, v_s, k_w, v_w, gates, cmp_pe, cmp_w):
    B, T, G, R, d = q.shape
    dt = q.dtype
    scale = d ** -0.5
    n_cmp = (T - CMP_LEN) // CMP_STRIDE + 1
    n_slc = T // SLC_LEN
    n_top = min(SLC_TOPN, n_slc)
    Qb = NSA_Q_BLOCK
    cmp_start = jnp.arange(n_cmp) * CMP_STRIDE
    cmp_end = cmp_start + CMP_LEN - 1
    blk_idx = cmp_start[:, None] + jnp.arange(CMP_LEN)[None, :]

    def compress(a, pe, w):
        blocks = a[:, blk_idx] + pe[None, None, :, None, :]
        return jnp.einsum('bnlgd,lde->bnge', blocks, w.reshape(CMP_LEN, d, d))

    kc = compress(k_c, cmp_pe[0], cmp_w[0])
    vc = compress(v_c, cmp_pe[1], cmp_w[1])
    slc_start = jnp.arange(n_slc) * SLC_LEN
    overlap = ((cmp_start[:, None] < slc_start[None, :] + SLC_LEN)
               & (cmp_end[:, None] >= slc_start[None, :])).astype(jnp.float32)
    ks_blocks = k_s.reshape(B, n_slc, SLC_LEN, G, d).transpose(0, 3, 1, 2, 4)
    vs_blocks = v_s.reshape(B, n_slc, SLC_LEN, G, d).transpose(0, 3, 1, 2, 4)
    gather_blocks = jax.vmap(jax.vmap(lambda blk, ix: blk[ix]))
    pad = ((0, 0), (WINDOW, 0), (0, 0), (0, 0))
    kw_p = jnp.pad(k_w, pad)
    vw_p = jnp.pad(v_w, pad)
    slc_ids = jnp.arange(n_slc)
    win_off = jnp.arange(WINDOW + Qb) - WINDOW

    def block(i):
        q0 = i * Qb
        qpos = q0 + jnp.arange(Qb)
        qb = lax.dynamic_slice_in_dim(q, q0, Qb, axis=1)
        gb = lax.dynamic_slice_in_dim(gates, q0, Qb, axis=1)
        s_c = jnp.einsum('bqgrd,bngd->bgrqn', qb, kc).astype(jnp.float32) * scale
        p_c = masked_softmax(s_c, cmp_end[None, :] <= qpos[:, None])
        o_c = jnp.einsum('bgrqn,bngd->bqgrd', p_c.astype(dt), vc)
        imp = jnp.einsum('bgrqn,nj->bgqj', p_c, overlap)
        cur = (qpos // SLC_LEN)[:, None]
        forced = (slc_ids[None, :] == 0) | (slc_ids[None, :] == cur) | (slc_ids[None, :] == cur - 1)
        imp = jnp.where(forced, FORCE_SCORE, jnp.where(slc_ids[None, :] <= cur, imp, -1.0))
        _, sel = lax.top_k(imp, n_top)
        kb = gather_blocks(ks_blocks, sel).reshape(B, G, Qb, n_top * SLC_LEN, d)
        vb = gather_blocks(vs_blocks, sel).reshape(B, G, Qb, n_top * SLC_LEN, d)
        tok = (sel[..., None] * SLC_LEN + jnp.arange(SLC_LEN)).reshape(B, G, 1, Qb, n_top * SLC_LEN)
        s_s = jnp.einsum('bqgrd,bgqmd->bgrqm', qb, kb).astype(jnp.float32) * scale
        p_s = masked_softmax(s_s, tok <= qpos[:, None])
        o_s = jnp.einsum('bgrqm,bgqmd->bqgrd', p_s.astype(dt), vb)
        kwb = lax.dynamic_slice_in_dim(kw_p, q0, WINDOW + Qb, axis=1)
        vwb = lax.dynamic_slice_in_dim(vw_p, q0, WINDOW + Qb, axis=1)
        kpos = q0 + win_off
        m_w = ((kpos[None, :] <= qpos[:, None]) & (kpos[None, :] > qpos[:, None] - WINDOW)
               & (kpos[None, :] >= 0))
        s_w = jnp.einsum('bqgrd,bkgd->bgrqk', qb, kwb).astype(jnp.float32) * scale
        p_w = masked_softmax(s_w, m_w)
        o_w = jnp.einsum('bgrqk,bkgd->bqgrd', p_w.astype(dt), vwb)
        return gb[..., 0:1] * o_c + gb[..., 1:2] * o_s + gb[..., 2:3] * o_w

    o = lax.map(block, jnp.arange(T // Qb))
    return jnp.moveaxis(o, 0, 1).reshape(B, T, G * R * d)


def odd_mixer(x, w_in, w_out, cmp_pe, cmp_w):
    B, T, _ = x.shape
    q, kc, vc, ks, vs, kw, vw, gl = split_cols(x @ w_in, ODD_SIZES)

    def kv(a):
        return a.reshape(B, T, NSA_GROUPS, NSA_HEAD_DIM)

    gates = jax.nn.sigmoid(gl.astype(jnp.float32)).astype(x.dtype).reshape(B, T, NSA_GROUPS, NSA_REP, 3)
    o = nsa(q.reshape(B, T, NSA_GROUPS, NSA_REP, NSA_HEAD_DIM), kv(kc), kv(vc), kv(ks), kv(vs),
            kv(kw), kv(vw), gates, cmp_pe, cmp_w)
    return o @ w_out


def memory_cross_attention(x, mem, wq, wkv, wo):
    B, T, _ = x.shape
    M = mem.shape[1]
    q = (x @ wq).reshape(B, T, XATTN_HEADS, XATTN_HEAD_DIM)
    kv = (mem @ wkv).reshape(B, M, 2, XATTN_HEADS, XATTN_HEAD_DIM)
    s = jnp.einsum('bthd,bmhd->bhtm', q, kv[:, :, 0]).astype(jnp.float32) * XATTN_HEAD_DIM ** -0.5
    p = jax.nn.softmax(s, axis=-1).astype(x.dtype)
    o = jnp.einsum('bhtm,bmhd->bthd', p, kv[:, :, 1]).reshape(B, T, XATTN_WIDTH)
    return o @ wo


def peer(x, wq, sub_keys, u, v):
    B, T, D = x.shape
    n_tok = B * T
    xt = x.reshape(n_tok, D)
    q = (xt @ wq).reshape(n_tok, PEER_HEADS, 2, PEER_QUERY_DIM // 2)
    s = jnp.einsum('nhcd,hckd->nhck', q, sub_keys).astype(jnp.float32)
    s1, i1 = lax.top_k(s[:, :, 0], PEER_TOPK)
    s2, i2 = lax.top_k(s[:, :, 1], PEER_TOPK)
    cand_s = (s1[..., :, None] + s2[..., None, :]).reshape(n_tok, PEER_HEADS, PEER_TOPK * PEER_TOPK)
    cand_id = (i1[..., :, None] * PEER_N_KEYS + i2[..., None, :]).reshape(n_tok, PEER_HEADS, PEER_TOPK * PEER_TOPK)
    top_s, pos = lax.top_k(cand_s, PEER_TOPK)
    ids = jnp.take_along_axis(cand_id, pos, axis=-1)
    g = jax.nn.softmax(top_s, axis=-1)
    n_chunk = n_tok // PEER_CHUNK
    n_sel = PEER_HEADS * PEER_TOPK

    def experts(args):
        xc, ic, gc = args
        h = jnp.einsum('td,ted->te', xc, u[ic]).astype(jnp.float32)
        a = (jax.nn.gelu(h, approximate=False) * gc).astype(xc.dtype)
        return jnp.einsum('te,ted->td', a, v[ic])

    y = lax.map(experts, (xt.reshape(n_chunk, PEER_CHUNK, D),
                          ids.reshape(n_chunk, PEER_CHUNK, n_sel),
                          g.reshape(n_chunk, PEER_CHUNK, n_sel)))
    return y.reshape(B, T, D)


def setup_inputs(seed: int = 0) -> dict:
    key = jax.random.key(seed)
    ks = jax.random.split(key, 21)
    D = D_MODEL

    def nrm(k, shape, scale):
        return jax.random.normal(k, shape, jnp.float32) * scale

    return {
        'x': nrm(ks[0], (BATCH, SEQ, D), 1.0),
        'mem': nrm(ks[1], (BATCH, MEM_LEN, D), 1.0),
        'w_in_even': nrm(ks[2], (N_EVEN, D, EVEN_IN), D ** -0.5),
        'w_out_even': nrm(ks[3], (N_EVEN, EVEN_MIX, D), DEEPNORM_BETA * EVEN_MIX ** -0.5),
        'diff_lambda': nrm(ks[4], (N_EVEN, 4, DIFF_HEAD_DIM), 0.1),
        'diff_norm_g': 1.0 + nrm(ks[5], (N_EVEN, DIFF_V_DIM), 0.02),
        'hgrn_gamma': nrm(ks[6], (N_EVEN, HGRN_KEY_WIDTH), 0.5),
        'hgrn_norm_g': 1.0 + nrm(ks[7], (N_EVEN, HGRN_VAL_DIM), 0.02),
        'w_in_odd': nrm(ks[8], (N_ODD, D, ODD_IN), D ** -0.5),
        'w_out_odd': nrm(ks[9], (N_ODD, NSA_WIDTH, D), DEEPNORM_BETA * NSA_WIDTH ** -0.5),
        'cmp_pe': nrm(ks[10], (N_ODD, 2, CMP_LEN, NSA_HEAD_DIM), 0.5),
        'cmp_w': nrm(ks[11], (N_ODD, 2, CMP_LEN * NSA_HEAD_DIM, NSA_HEAD_DIM), (CMP_LEN * NSA_HEAD_DIM) ** -0.5),
        'xattn_wq': nrm(ks[12], (DEPTH, D, XATTN_WIDTH), D ** -0.5),
        'xattn_wkv': nrm(ks[13], (DEPTH, D, 2 * XATTN_WIDTH), D ** -0.5),
        'xattn_wo': nrm(ks[14], (DEPTH, XATTN_WIDTH, D), DEEPNORM_BETA * XATTN_WIDTH ** -0.5),
        'ln_g': 1.0 + nrm(ks[15], (DEPTH, 3, D), 0.02),
        'ln_b': nrm(ks[16], (DEPTH, 3, D), 0.02),
        'peer_wq': nrm(ks[17], (DEPTH, D, PEER_HEADS * PEER_QUERY_DIM), D ** -0.5),
        'peer_keys': nrm(ks[18], (DEPTH, PEER_HEADS, 2, PEER_N_KEYS, PEER_QUERY_DIM // 2), (PEER_QUERY_DIM // 2) ** -0.5),
        'peer_u': nrm(ks[19], (DEPTH, PEER_N_EXPERTS, D), D ** -0.5),
        'peer_v': nrm(ks[20], (DEPTH, PEER_N_EXPERTS, D), DEEPNORM_BETA * PEER_HEADS ** -0.5),
    }


def reference(x, mem, w_in_even, w_out_even, diff_lambda, diff_norm_g, hgrn_gamma, hgrn_norm_g,
              w_in_odd, w_out_odd, cmp_pe, cmp_w, xattn_wq, xattn_wkv, xattn_wo, ln_g, ln_b,
              peer_wq, peer_keys, peer_u, peer_v):
    p_lb = jax.nn.softmax(hgrn_gamma.astype(jnp.float32), axis=0)
    lower_bounds = jnp.cumsum(p_lb, axis=0) - p_lb[0]
    h = x
    for l in range(DEPTH):
        j = l // 2
        if l % 2 == 0:
            lam_init = 0.8 - 0.6 * math.exp(-0.3 * l)
            y = even_mixer(h, w_in_even[j], w_out_even[j], diff_lambda[j], diff_norm_g[j],
                           lower_bounds[j], hgrn_norm_g[j], lam_init)
        else:
            y = odd_mixer(h, w_in_odd[j], w_out_odd[j], cmp_pe[j], cmp_w[j])
        h = layer_norm(DEEPNORM_ALPHA * h + y, ln_g[l, 0], ln_b[l, 0])
        h = layer_norm(DEEPNORM_ALPHA * h + memory_cross_attention(h, mem, xattn_wq[l], xattn_wkv[l], xattn_wo[l]),
                       ln_g[l, 1], ln_b[l, 1])
        h = layer_norm(DEEPNORM_ALPHA * h + peer(h, peer_wq[l], peer_keys[l], peer_u[l], peer_v[l]),
                       ln_g[l, 2], ln_b[l, 2])
    return h
```

```python
import functools
import math

import jax
import jax.numpy as jnp
from jax import lax
from jax.experimental import pallas as pl
from jax.experimental.pallas import tpu as pltpu

F32 = jnp.float32
BF16 = jnp.bfloat16
I32 = jnp.int32

LANES = 128
VMEM_LIMIT_BYTES = 56 * 1024 * 1024

DIFF_HEADS = 8
HGRN_HEADS = 16
HGRN_CHUNK = 16
LB_FLOOR = 1e-30
NSA_GROUPS = 4
NSA_REP = 8
CMP_LEN = 32
CMP_STRIDE = 16
SLC_LEN = 64
SLC_TOPN = 16
WINDOW = 512
FORCE_SCORE = 1e9
XATTN_HEADS = 4
PEER_HEADS = 8
PEER_N_KEYS = 128
PEER_TOPK = 16
LN_EPS = 1e-5
RMS_EPS = 1e-6
NEG_INF = -1e30

_NT = (((1,), (1,)), ((), ()))
_TN = (((0,), (0,)), ((), ()))


def _params(*sem):
    return pltpu.CompilerParams(dimension_semantics=sem, vmem_limit_bytes=VMEM_LIMIT_BYTES)


def _dot(a, b, dims=None, precision=None):
    if dims is None:
        return jnp.dot(a, b, preferred_element_type=F32, precision=precision)
    return lax.dot_general(a, b, dims, preferred_element_type=F32, precision=precision)


def _matmul_kernel(x_ref, w_ref, o_ref, *, act):
    acc = _dot(x_ref[...], w_ref[...])
    if act == "sigmoid":
        acc = jax.nn.sigmoid(acc)
    o_ref[...] = acc.astype(o_ref.dtype)


def matmul(x, w, out_dtype, act=None, tm=512, tn=512):
    m, k = x.shape
    n = w.shape[1]
    tm = min(tm, m)
    tn = min(tn, n)
    assert m % tm == 0 and n % tn == 0, (m, n, tm, tn)
    return pl.pallas_call(
        functools.partial(_matmul_kernel, act=act),
        grid=(m // tm, n // tn),
        in_specs=[pl.BlockSpec((tm, k), lambda i, j: (i, 0)),
                  pl.BlockSpec((k, tn), lambda i, j: (0, j))],
        out_specs=pl.BlockSpec((tm, tn), lambda i, j: (i, j)),
        out_shape=jax.ShapeDtypeStruct((m, n), out_dtype),
        compiler_params=_params("parallel", "parallel"),
        name="matmul",
    )(x, w)


def _layer_norm_rows(z, g, b):
    mu = jnp.mean(z, axis=-1, keepdims=True)
    zc = z - mu
    var = jnp.mean(zc * zc, axis=-1, keepdims=True)
    return zc * lax.rsqrt(var + LN_EPS) * g + b


def _ln_res_kernel(h_ref, y_ref, g_ref, b_ref, o_ref, ob_ref, *, alpha, y_transposed):
    y = y_ref[...]
    if y_transposed:
        y = y.T
    out = _layer_norm_rows(alpha * h_ref[...] + y, g_ref[...], b_ref[...])
    o_ref[...] = out
    ob_ref[...] = out.astype(BF16)


def ln_residual(h, y, g, b, alpha, y_transposed=False, tm=256):
    n, d = h.shape
    tm = min(tm, n)
    y_spec = (pl.BlockSpec((d, tm), lambda i: (0, i)) if y_transposed
              else pl.BlockSpec((tm, d), lambda i: (i, 0)))
    return pl.pallas_call(
        functools.partial(_ln_res_kernel, alpha=alpha, y_transposed=y_transposed),
        grid=(n // tm,),
        in_specs=[pl.BlockSpec((tm, d), lambda i: (i, 0)), y_spec,
                  pl.BlockSpec((1, d), lambda i: (0, 0)),
                  pl.BlockSpec((1, d), lambda i: (0, 0))],
        out_specs=[pl.BlockSpec((tm, d), lambda i: (i, 0)),
                   pl.BlockSpec((tm, d), lambda i: (i, 0))],
        out_shape=[jax.ShapeDtypeStruct((n, d), F32), jax.ShapeDtypeStruct((n, d), BF16)],
        compiler_params=_params("parallel"),
        name="ln_residual",
    )(h, y, g.reshape(1, d), b.reshape(1, d))


def _diff_attn_kernel(lam_ref, q_ref, k_ref, v_ref, g_ref, o_ref, m_ref, l_ref, acc_ref,
                      *, tq, tk, lam_init):
    q0 = pl.program_id(2) * tq
    scale = LANES ** -0.5
    m_ref[...] = jnp.full(m_ref.shape, NEG_INF, F32)
    l_ref[...] = jnp.zeros(l_ref.shape, F32)
    acc_ref[...] = jnp.zeros(acc_ref.shape, F32)
    qpos = q0 + lax.broadcasted_iota(I32, (tq, 1), 0)
    q = q_ref[...]

    def body(c, carry):
        k0 = pl.multiple_of(c * tk, tk)
        k = k_ref[pl.ds(k0, tk), :]
        v = v_ref[pl.ds(k0, tk), :]
        mask = (k0 + lax.broadcasted_iota(I32, (1, tk), 1)) <= qpos
        for half in range(2):
            sl = slice(half * LANES, (half + 1) * LANES)
            s = _dot(q[:, sl], k[:, sl], _NT) * scale
            s = jnp.where(mask, s, NEG_INF)
            m_old = m_ref[half]
            m_new = jnp.maximum(m_old, jnp.max(s, axis=-1, keepdims=True))
            a = jnp.exp(m_old - m_new)
            p = jnp.exp(s - m_new)
            l_ref[half] = a * l_ref[half] + jnp.sum(p, axis=-1, keepdims=True)
            acc_ref[half] = a * acc_ref[half] + _dot(p.astype(BF16), v)
            m_ref[half] = m_new
        return carry

    lax.fori_loop(0, (q0 + tq + tk - 1) // tk, body, 0)

    lv = lam_ref[...]
    lam = (jnp.exp(jnp.sum(lv[0:1] * lv[1:2], axis=-1, keepdims=True))
           - jnp.exp(jnp.sum(lv[2:3] * lv[3:4], axis=-1, keepdims=True)) + lam_init)
    o = acc_ref[0] / l_ref[0] - lam * (acc_ref[1] / l_ref[1])
    o = o * lax.rsqrt(jnp.mean(o * o, axis=-1, keepdims=True) + RMS_EPS)
    o_ref[...] = (o * g_ref[...] * (1.0 - lam_init)).astype(o_ref.dtype)


def diff_attention(z, lam_vec, norm_g, batch, seq, lam_init, tq=256, tk=512):
    dv = 2 * LANES
    tq = min(tq, seq)
    tk = min(tk, seq)
    nq = seq // tq
    return pl.pallas_call(
        functools.partial(_diff_attn_kernel, tq=tq, tk=tk, lam_init=lam_init),
        grid=(batch, DIFF_HEADS, nq),
        in_specs=[pl.BlockSpec((4, LANES), lambda b, h, i: (0, 0)),
                  pl.BlockSpec((tq, dv), lambda b, h, i: (b * nq + i, h)),
                  pl.BlockSpec((seq, dv), lambda b, h, i: (b, DIFF_HEADS + h)),
                  pl.BlockSpec((seq, dv), lambda b, h, i: (b, 2 * DIFF_HEADS + h)),
                  pl.BlockSpec((1, dv), lambda b, h, i: (0, 0))],
        out_specs=pl.BlockSpec((tq, dv), lambda b, h, i: (b * nq + i, h)),
        out_shape=jax.ShapeDtypeStruct((batch * seq, DIFF_HEADS * dv), BF16),
        scratch_shapes=[pltpu.VMEM((2, tq, 1), F32), pltpu.VMEM((2, tq, 1), F32),
                        pltpu.VMEM((2, tq, dv), F32)],
        compiler_params=_params("parallel", "parallel", "arbitrary"),
        name="diff_attention",
    )(lam_vec, z, z, z, norm_g.reshape(1, dv))


def _hgrn_kernel(gam_ref, q_ref, f_ref, i_ref, gate_ref, ng_ref, o_ref,
                 st_ref, b_ref, qh_ref, kin_ref, *, tt, layer):
    c_len = HGRN_CHUNK

    @pl.when(pl.program_id(2) == 0)
    def _():
        st_ref[...] = jnp.zeros(st_ref.shape, F32)

    gam = gam_ref[...]
    e = jnp.exp(gam - jnp.max(gam, axis=0, keepdims=True))
    p = e / jnp.sum(e, axis=0, keepdims=True)
    lb = jnp.zeros((1, LANES), F32)
    for r in range(1, layer + 1):
        lb = lb + p[r:r + 1]
    log_lb = jnp.log(jnp.maximum(lb, LB_FLOOR))
    z = f_ref[...]
    log_sig = jnp.minimum(z, 0.0) - jnp.log1p(jnp.exp(-jnp.abs(z)))
    log_gate = jnp.log1p(-lb) + log_sig
    log_f = (jnp.maximum(log_lb, log_gate)
             + jnp.log1p(jnp.exp(-jnp.abs(log_lb - log_gate))))
    kin_ref[...] = (1.0 - lb) * jax.nn.sigmoid(-z)
    qh_ref[...] = jax.nn.silu(q_ref[...])
    row = lax.broadcasted_iota(I32, (tt, 1), 0) % c_len
    b = log_f
    step = 1
    while step < c_len:
        b = b + jnp.where(row >= step, pltpu.roll(b, step, axis=0), 0.0)
        step *= 2
    b_ref[...] = b
    trow = lax.broadcasted_iota(I32, (c_len, 1), 0)
    ng = ng_ref[...]

    def body(c, carry):
        r0 = pl.multiple_of(c * c_len, c_len)
        bc = b_ref[pl.ds(r0, c_len), :]
        qc = qh_ref[pl.ds(r0, c_len), :]
        kc = kin_ref[pl.ds(r0, c_len), :]
        ic = i_ref[pl.ds(r0, c_len), :]
        st = st_ref[...]
        o = _dot((qc * jnp.exp(bc)).astype(BF16), st.astype(BF16), _NT)
        for s in range(c_len):
            es = jnp.where(trow >= s, jnp.exp(bc - bc[s:s + 1]), 0.0)
            col = jnp.sum(qc * es * kc[s:s + 1], axis=-1, keepdims=True)
            o = o + col * ic[s:s + 1]
        b_last = bc[c_len - 1:c_len]
        kd = kc * jnp.exp(b_last - bc)
        st_ref[...] = st * jnp.exp(b_last) + _dot(ic.astype(BF16), kd.astype(BF16), _TN)
        y = o * lax.rsqrt(jnp.mean(o * o, axis=-1, keepdims=True) + RMS_EPS) * ng
        gate = jax.nn.silu(gate_ref[pl.ds(r0, c_len), :])
        o_ref[pl.ds(r0, c_len), :] = (y * gate).astype(o_ref.dtype)
        return carry

    lax.fori_loop(0, tt // c_len, body, 0)


def hgrn2(z, gamma, norm_g, batch, seq, layer, tt=512):
    tt = min(tt, seq)
    nt = seq // tt
    n_even = gamma.shape[0]
    hh = HGRN_HEADS

    def col(offset):
        return pl.BlockSpec((tt, LANES), lambda b, h, t: (b * nt + t, offset * hh + h))

    return pl.pallas_call(
        functools.partial(_hgrn_kernel, tt=tt, layer=layer),
        grid=(batch, hh, nt),
        in_specs=[pl.BlockSpec((n_even, LANES), lambda b, h, t: (0, h)),
                  col(0), col(1), col(2), col(3),
                  pl.BlockSpec((1, LANES), lambda b, h, t: (0, 0))],
        out_specs=pl.BlockSpec((tt, LANES), lambda b, h, t: (b * nt + t, h)),
        out_shape=jax.ShapeDtypeStruct((batch * seq, hh * LANES), BF16),
        scratch_shapes=[pltpu.VMEM((LANES, LANES), F32), pltpu.VMEM((tt, LANES), F32),
                        pltpu.VMEM((tt, LANES), F32), pltpu.VMEM((tt, LANES), F32)],
        compiler_params=_params("parallel", "parallel", "arbitrary"),
        name="hgrn2",
    )(gamma, z, z, z, z, norm_g.reshape(1, LANES))


def _topk_rows(vals, k):
    r, n = vals.shape
    iota0 = lax.broadcasted_iota(I32, (r, n), 0)
    rank = jnp.full((r, n), k, I32)
    tops = []
    for a in range(k):
        m = jnp.max(vals, axis=0, keepdims=True)
        idx = jnp.min(jnp.where(vals == m, iota0, r), axis=0, keepdims=True)
        hit = iota0 == idx
        rank = jnp.where(hit, a, rank)
        tops.append(m)
        vals = jnp.where(hit, -jnp.inf, vals)
    return tops, rank


def _compress_kernel(x_ref, w_ref, pe_ref, o_ref):
    pq = _dot(x_ref[...], w_ref[...])
    nrow = pq.shape[0]
    pe_c = _dot(pe_ref[...], w_ref[...])
    const = pe_c[0:1, :LANES] + pe_c[1:2, LANES:]
    nxt = pltpu.roll(pq[:, LANES:], nrow - 1, axis=0)
    o_ref[...] = (pq[:, :LANES] + nxt + const).astype(o_ref.dtype)


def nsa_compress(xc, wcat, pecat):
    bsz, nj, nrow, kk = xc.shape
    return pl.pallas_call(
        _compress_kernel,
        grid=(bsz, nj),
        in_specs=[pl.BlockSpec((None, None, nrow, kk), lambda b, j: (b, j, 0, 0)),
                  pl.BlockSpec((None, kk, 2 * LANES), lambda b, j: (j // NSA_GROUPS, 0, 0)),
                  pl.BlockSpec((None, 8, kk), lambda b, j: (j // NSA_GROUPS, 0, 0))],
        out_specs=pl.BlockSpec((None, None, nrow, LANES), lambda b, j: (b, j, 0, 0)),
        out_shape=jax.ShapeDtypeStruct((bsz, nj, nrow, LANES), BF16),
        compiler_params=_params("parallel", "parallel"),
        name="nsa_compress",
    )(xc, wcat, pecat)


def _nsa_kernel(q_ref, kc_ref, vc_ref, ks_ref, vs_ref, kw_ref, vw_ref, gt_ref, o_ref,
                m_ref, l_ref, acc_ref, *, tq, tk, ncp, nsp, n_top):
    rep = NSA_REP
    rows = rep * tq
    q0 = pl.program_id(2) * tq
    scale = LANES ** -0.5
    qb = q_ref[...]
    qs = jnp.concatenate([qb[:, r * LANES:(r + 1) * LANES] for r in range(rep)], axis=0)
    qpos = q0 + lax.broadcasted_iota(I32, (tq, 1), 0)

    s_c = (_dot(qs, kc_ref[...], _NT) * scale).reshape(rep, tq, ncp)
    cend = lax.broadcasted_iota(I32, (1, ncp), 1) * CMP_STRIDE + (CMP_LEN - 1)
    mask_c = (cend <= qpos)[None]
    s_c = jnp.where(mask_c, s_c, NEG_INF)
    e_c = jnp.where(mask_c, jnp.exp(s_c - jnp.max(s_c, axis=-1, keepdims=True)), 0.0)
    p_c = e_c / jnp.maximum(jnp.sum(e_c, axis=-1, keepdims=True), 1e-30)
    o_c = _dot(p_c.reshape(rows, ncp).astype(BF16), vc_ref[...])

    n_i = lax.broadcasted_iota(I32, (ncp, nsp), 0)
    j4 = lax.broadcasted_iota(I32, (ncp, nsp), 1) * (SLC_LEN // CMP_STRIDE)
    overlap = ((n_i >= j4 - (CMP_LEN - 1) // CMP_STRIDE) & (n_i < j4 + SLC_LEN // CMP_STRIDE)).astype(F32)
    imp = _dot(jnp.sum(p_c, axis=0), overlap, precision=lax.Precision.HIGHEST)
    imp_t = imp.T
    jt = lax.broadcasted_iota(I32, (nsp, 1), 0)
    cur = (q0 + lax.broadcasted_iota(I32, (1, tq), 1)) // SLC_LEN
    forced = (jt == 0) | (jt == cur) | (jt == cur - 1)
    imp_t = jnp.where(forced, FORCE_SCORE, jnp.where(jt <= cur, imp_t, -1.0))
    _, rank = _topk_rows(imp_t, n_top)
    sel = (rank < n_top).astype(F32).T.astype(BF16)

    m_ref[...] = jnp.full(m_ref.shape, NEG_INF, F32)
    l_ref[...] = jnp.zeros(l_ref.shape, F32)
    acc_ref[...] = jnp.zeros(acc_ref.shape, F32)
    jj = lax.broadcasted_iota(I32, (nsp, 1), 0)

    def body(c, carry):
        k0 = pl.multiple_of(c * tk, tk)
        k = ks_ref[pl.ds(k0, tk), :]
        v = vs_ref[pl.ds(k0, tk), :]
        kpos = k0 + lax.broadcasted_iota(I32, (1, tk), 1)
        expand = (jj == kpos // SLC_LEN).astype(BF16)
        mask = ((_dot(sel, expand) > 0.5) & (kpos <= qpos))[None]
        s = jnp.where(mask, (_dot(qs, k, _NT) * scale).reshape(rep, tq, tk), NEG_INF)
        m_old = m_ref[...]
        m_new = jnp.maximum(m_old, jnp.max(s, axis=-1, keepdims=True))
        a = jnp.exp(m_old - m_new)
        p = jnp.exp(s - m_new)
        l_ref[...] = a * l_ref[...] + jnp.sum(p, axis=-1, keepdims=True)
        pv = _dot(p.reshape(rows, tk).astype(BF16), v).reshape(rep, tq, LANES)
        acc_ref[...] = a * acc_ref[...] + pv
        m_ref[...] = m_new
        return carry

    lax.fori_loop(0, (q0 + tq + tk - 1) // tk, body, 0)
    o_s = (acc_ref[...] / l_ref[...]).reshape(rows, LANES)

    span = WINDOW + tq
    start = pl.multiple_of(jnp.maximum(q0 - WINDOW, 0), tq)
    kpos = start + lax.broadcasted_iota(I32, (1, span), 1)
    mask_w = ((kpos <= qpos) & (kpos > qpos - WINDOW))[None]
    s_w = (_dot(qs, kw_ref[pl.ds(start, span), :], _NT) * scale).reshape(rep, tq, span)
    s_w = jnp.where(mask_w, s_w, NEG_INF)
    e_w = jnp.exp(s_w - jnp.max(s_w, axis=-1, keepdims=True))
    p_w = e_w / jnp.sum(e_w, axis=-1, keepdims=True)
    o_w = _dot(p_w.reshape(rows, span).astype(BF16), vw_ref[pl.ds(start, span), :])

    gt = gt_ref[...]
    o = gt[:, 0:1] * o_c + gt[:, 1:2] * o_s + gt[:, 2:3] * o_w
    o_ref[...] = jnp.concatenate([o[r * tq:(r + 1) * tq] for r in range(rep)],
                                 axis=1).astype(o_ref.dtype)


def nsa_attention(z, kvc, gates_t, batch, seq, tq=128, tk=256):
    assert seq % tq == 0 and seq >= WINDOW + tq and tq % SLC_LEN == 0
    nq = seq // tq
    ncp = seq // CMP_STRIDE
    n_slc = seq // SLC_LEN
    nsp = -(-n_slc // LANES) * LANES
    n_top = min(SLC_TOPN, n_slc)
    qw = NSA_REP * LANES
    g = NSA_GROUPS
    q_cols = g * NSA_REP

    def kv_col(which):
        return pl.BlockSpec((seq, LANES), lambda b, gi, i: (b, q_cols + which * g + gi))

    return pl.pallas_call(
        functools.partial(_nsa_kernel, tq=tq, tk=tk, ncp=ncp, nsp=nsp, n_top=n_top),
        grid=(batch, g, nq),
        in_specs=[pl.BlockSpec((tq, qw), lambda b, gi, i: (b * nq + i, gi)),
                  pl.BlockSpec((None, None, ncp, LANES), lambda b, gi, i: (b, gi, 0, 0)),
                  pl.BlockSpec((None, None, ncp, LANES), lambda b, gi, i: (b, g + gi, 0, 0)),
                  kv_col(2), kv_col(3), kv_col(4), kv_col(5),
                  pl.BlockSpec((None, None, None, NSA_REP * tq, 3),
                               lambda b, gi, i: (b, gi, i, 0, 0))],
        out_specs=pl.BlockSpec((tq, qw), lambda b, gi, i: (b * nq + i, gi)),
        out_shape=jax.ShapeDtypeStruct((batch * seq, g * qw), BF16),
        scratch_shapes=[pltpu.VMEM((NSA_REP, tq, 1), F32), pltpu.VMEM((NSA_REP, tq, 1), F32),
                        pltpu.VMEM((NSA_REP, tq, LANES), F32)],
        compiler_params=_params("parallel", "parallel", "arbitrary"),
        name="nsa_attention",
    )(z, kvc, kvc, z, z, z, z, gates_t)


def _xattn_kernel(hb_ref, h_ref, wq_ref, kv_ref, wo_ref, g_ref, b_ref, o_ref, ob_ref, *, alpha):
    width = XATTN_HEADS * LANES
    scale = LANES ** -0.5
    q = _dot(hb_ref[...], wq_ref[...]).astype(BF16)
    kv = kv_ref[...]
    outs = []
    for hd in range(XATTN_HEADS):
        sl = slice(hd * LANES, (hd + 1) * LANES)
        s = _dot(q[:, sl], kv[:, sl], _NT) * scale
        e = jnp.exp(s - jnp.max(s, axis=-1, keepdims=True))
        p = e / jnp.sum(e, axis=-1, keepdims=True)
        outs.append(_dot(p.astype(BF16), kv[:, width + hd * LANES:width + (hd + 1) * LANES]))
    o = jnp.concatenate(outs, axis=1).astype(BF16)
    y = _dot(o, wo_ref[...])
    out = _layer_norm_rows(alpha * h_ref[...] + y, g_ref[...], b_ref[...])
    o_ref[...] = out
    ob_ref[...] = out.astype(BF16)


def xattn_block(hb, h, wq, kv, wo, g, b, alpha, batch, seq, tm=256):
    n, d = h.shape
    tm = min(tm, seq)
    nt = seq // tm
    mem_len = kv.shape[0] // batch
    width = XATTN_HEADS * LANES
    return pl.pallas_call(
        functools.partial(_xattn_kernel, alpha=alpha),
        grid=(batch, nt),
        in_specs=[pl.BlockSpec((tm, d), lambda bi, i: (bi * nt + i, 0)),
                  pl.BlockSpec((tm, d), lambda bi, i: (bi * nt + i, 0)),
                  pl.BlockSpec((d, width), lambda bi, i: (0, 0)),
                  pl.BlockSpec((mem_len, 2 * width), lambda bi, i: (bi, 0)),
                  pl.BlockSpec((width, d), lambda bi, i: (0, 0)),
                  pl.BlockSpec((1, d), lambda bi, i: (0, 0)),
                  pl.BlockSpec((1, d), lambda bi, i: (0, 0))],
        out_specs=[pl.BlockSpec((tm, d), lambda bi, i: (bi * nt + i, 0)),
                   pl.BlockSpec((tm, d), lambda bi, i: (bi * nt + i, 0))],
        out_shape=[jax.ShapeDtypeStruct((n, d), F32), jax.ShapeDtypeStruct((n, d), BF16)],
        compiler_params=_params("parallel", "parallel"),
        name="xattn_block",
    )(hb, h, wq, kv, wo, g.reshape(1, d), b.reshape(1, d))


def _peer_route_kernel(q_ref, keys_ref, mb_ref, e1_ref, b2_ref, e2_ref, *, tn):
    k = PEER_TOPK
    nk = PEER_N_KEYS
    hi = lax.Precision.HIGHEST
    for h in range(PEER_HEADS):
        s1 = _dot(keys_ref[h, 0], q_ref[:, (2 * h) * LANES:(2 * h + 1) * LANES], _NT, hi)
        s2 = _dot(keys_ref[h, 1], q_ref[:, (2 * h + 1) * LANES:(2 * h + 2) * LANES], _NT, hi)
        v1, rank1 = _topk_rows(s1, k)
        v2, rank2 = _topk_rows(s2, k)
        v2_all = jnp.concatenate(v2, axis=0)
        cand = jnp.concatenate([v1[a] + v2_all for a in range(k)], axis=0)
        tops, crank = _topk_rows(cand, k)
        zsum = tops[0] * 0.0
        for a in range(k):
            zsum = zsum + jnp.exp(tops[a] - tops[0])
        brow = lax.broadcasted_iota(I32, (k, tn), 0)
        bit = jnp.left_shift(1, brow).astype(F32)
        mb = jnp.zeros((nk, tn), I32)
        b2 = jnp.zeros((nk, tn), I32)
        for a in range(k):
            taken = jnp.where(crank[a * k:(a + 1) * k] < k, bit, 0.0)
            ma = jnp.sum(taken, axis=0, keepdims=True).astype(I32)
            mb = jnp.where(rank1 == a, ma, mb)
            b2 = jnp.where(rank2 == a, 1 << a, b2)
        mb_ref[h] = mb
        b2_ref[h] = b2
        e1_ref[h] = jnp.where(rank1 < k, jnp.exp(s1 - v1[0]), 0.0) / zsum
        e2_ref[h] = jnp.where(rank2 < k, jnp.exp(s2 - v2[0]), 0.0)


def peer_route(q, keys, tn=256):
    n = q.shape[0]
    tn = min(tn, n)
    tab = lambda dt: jax.ShapeDtypeStruct((PEER_HEADS, PEER_N_KEYS, n), dt)
    tspec = pl.BlockSpec((PEER_HEADS, PEER_N_KEYS, tn), lambda i: (0, 0, i))
    return pl.pallas_call(
        functools.partial(_peer_route_kernel, tn=tn),
        grid=(n // tn,),
        in_specs=[pl.BlockSpec((tn, q.shape[1]), lambda i: (i, 0)),
                  pl.BlockSpec(keys.shape, lambda i: (0, 0, 0, 0))],
        out_specs=[tspec, tspec, tspec, tspec],
        out_shape=[tab(I32), tab(F32), tab(I32), tab(F32)],
        compiler_params=_params("parallel"),
        name="peer_route",
    )(q, keys)


def _peer_dense_kernel(x_ref, u_ref, vt_ref, mb_ref, e1_ref, b2_ref, e2_ref, o_ref, *, te, tn):
    j = pl.program_id(1)

    @pl.when(j == 0)
    def _():
        o_ref[...] = jnp.zeros(o_ref.shape, F32)

    ht = _dot(u_ref[...], x_ref[...], _NT)
    parts = []
    for sub in range(te // PEER_N_KEYS):
        i1 = j * (te // PEER_N_KEYS) + sub
        w = jnp.zeros((PEER_N_KEYS, tn), F32)
        for h in range(PEER_HEADS):
            hit = (mb_ref[h, pl.ds(i1, 1), :] & b2_ref[h]) != 0
            w = w + jnp.where(hit, e1_ref[h, pl.ds(i1, 1), :] * e2_ref[h], 0.0)
        hh = ht[sub * PEER_N_KEYS:(sub + 1) * PEER_N_KEYS]
        gelu = 0.5 * hh * (1.0 + lax.erf(hh * (2.0 ** -0.5)))
        parts.append((gelu * w).astype(BF16))
    a = parts[0] if len(parts) == 1 else jnp.concatenate(parts, axis=0)
    o_ref[...] += _dot(vt_ref[...], a)


def peer_dense(xb, u, vt, tables, tn=512, te=256):
    n, d = xb.shape
    n_exp = u.shape[0]
    tn = min(tn, n)
    tspec = pl.BlockSpec((PEER_HEADS, PEER_N_KEYS, tn), lambda i, j: (0, 0, i))
    return pl.pallas_call(
        functools.partial(_peer_dense_kernel, te=te, tn=tn),
        grid=(n // tn, n_exp // te),
        in_specs=[pl.BlockSpec((tn, d), lambda i, j: (i, 0)),
                  pl.BlockSpec((te, d), lambda i, j: (j, 0)),
                  pl.BlockSpec((d, te), lambda i, j: (0, j)),
                  tspec, tspec, tspec, tspec],
        out_specs=pl.BlockSpec((d, tn), lambda i, j: (0, i)),
        out_shape=jax.ShapeDtypeStruct((d, n), F32),
        compiler_params=_params("parallel", "arbitrary"),
        name="peer_dense",
    )(xb, u, vt, *tables)


def odd_mixer_core(hb, w_in, cmp_pe, cmp_w, batch, seq, nsa_tq=128):
    q_w = NSA_GROUPS * NSA_REP * LANES
    kv_w = NSA_GROUPS * LANES
    nsa_main = q_w + 6 * kv_w
    n_gate = 3 * NSA_GROUPS * NSA_REP
    z = matmul(hb, w_in[:, :nsa_main].astype(BF16), BF16)
    w_gate = jnp.pad(w_in[:, nsa_main:], ((0, 0), (0, LANES - n_gate))).astype(BF16)
    gates = matmul(hb, w_gate, F32, act="sigmoid")[:, :n_gate]
    gates_t = gates.reshape(batch, seq // nsa_tq, nsa_tq, NSA_GROUPS, NSA_REP, 3)
    gates_t = gates_t.transpose(0, 3, 1, 4, 2, 5).reshape(
        batch, NSA_GROUPS, seq // nsa_tq, NSA_REP * nsa_tq, 3)
    xc = z[:, q_w:q_w + 2 * kv_w].reshape(batch, seq // CMP_STRIDE, CMP_STRIDE, 2 * NSA_GROUPS, LANES)
    xc = xc.transpose(0, 3, 1, 2, 4).reshape(
        batch, 2 * NSA_GROUPS, seq // CMP_STRIDE, CMP_STRIDE * LANES)
    half = CMP_STRIDE * LANES
    cw = cmp_w.astype(BF16)
    wcat = jnp.concatenate([cw[:, :half], cw[:, half:]], axis=2)
    pe = cmp_pe.reshape(2, 2, 1, half).astype(BF16)
    pecat = jnp.concatenate([pe[:, 0], pe[:, 1], jnp.zeros((2, 6, half), BF16)], axis=1)
    kvc = nsa_compress(xc, wcat, pecat)
    return nsa_attention(z, kvc, gates_t, batch, seq, tq=nsa_tq)


def kernel(x, mem, w_in_even, w_out_even, diff_lambda, diff_norm_g, hgrn_gamma, hgrn_norm_g,
           w_in_odd, w_out_odd, cmp_pe, cmp_w, xattn_wq, xattn_wkv, xattn_wo, ln_g, ln_b,
           peer_wq, peer_keys, peer_u, peer_v):
    batch, seq, d = x.shape
    depth = ln_g.shape[0]
    n = batch * seq
    assert d == 32 * LANES, "head dims are d_model/32 and must equal the lane count"
    alpha = (2.0 * depth) ** 0.25
    diff_w = DIFF_HEADS * 2 * LANES
    attn_cols = 3 * diff_w

    h = x.reshape(n, d)
    hb = h.astype(BF16)
    memb = mem.reshape(-1, d).astype(BF16)

    for l in range(depth):
        j = l // 2
        if l % 2 == 0:
            lam_init = 0.8 - 0.6 * math.exp(-0.3 * l)
            w_in = w_in_even[j].astype(BF16)
            z_attn = matmul(hb, w_in[:, :attn_cols], BF16)
            z_rec = matmul(hb, w_in[:, attn_cols:], F32)
            o_a = diff_attention(z_attn, diff_lambda[j], diff_norm_g[j], batch, seq, lam_init)
            o_b = hgrn2(z_rec, hgrn_gamma, hgrn_norm_g[j], batch, seq, j)
            mix = jnp.concatenate([o_a, o_b], axis=1)
            y = matmul(mix, w_out_even[j].astype(BF16), F32)
        else:
            o = odd_mixer_core(hb, w_in_odd[j], cmp_pe[j], cmp_w[j], batch, seq)
            y = matmul(o, w_out_odd[j].astype(BF16), F32)
        h, hb = ln_residual(h, y, ln_g[l, 0], ln_b[l, 0], alpha)

        kv = matmul(memb, xattn_wkv[l].astype(BF16), BF16)
        h, hb = xattn_block(hb, h, xattn_wq[l].astype(BF16), kv, xattn_wo[l].astype(BF16),
                            ln_g[l, 1], ln_b[l, 1], alpha, batch, seq)

        pq = matmul(hb, peer_wq[l].astype(BF16), F32)
        tables = peer_route(pq, peer_keys[l])
        yt = peer_dense(hb, peer_u[l].astype(BF16), peer_v[l].T.astype(BF16), tables)
        h, hb = ln_residual(h, yt, ln_g[l, 2], ln_b[l, 2], alpha, y_transposed=True)

    return h.reshape(batch, seq, d)
```

```python
import functools
import math

import jax
import jax.numpy as jnp
from jax import lax
from jax.experimental import pallas as pl
from jax.experimental.pallas import tpu as pltpu

F32 = jnp.float32
BF16 = jnp.bfloat16
I32 = jnp.int32

LANES = 128
VMEM_LIMIT_BYTES = 56 * 1024 * 1024

DIFF_HEADS = 8
HGRN_HEADS = 16
HGRN_CHUNK = 16
LB_FLOOR = 1e-30
NSA_GROUPS = 4
NSA_REP = 8
CMP_LEN = 32
CMP_STRIDE = 16
SLC_LEN = 64
SLC_TOPN = 16
WINDOW = 512
FORCE_SCORE = 1e9
XATTN_HEADS = 4
PEER_HEADS = 8
PEER_N_KEYS = 128
PEER_TOPK = 16
PEER_CHAIN = 256
LN_EPS = 1e-5
RMS_EPS = 1e-6
NEG_INF = -1e30

_NT = (((1,), (1,)), ((), ()))
_TN = (((0,), (0,)), ((), ()))


def _params(*sem):
    return pltpu.CompilerParams(dimension_semantics=sem, vmem_limit_bytes=VMEM_LIMIT_BYTES)


def _dot(a, b, dims=None, precision=None):
    if dims is None:
        return jnp.dot(a, b, preferred_element_type=F32, precision=precision)
    return lax.dot_general(a, b, dims, preferred_element_type=F32, precision=precision)


def _matmul_kernel(x_ref, w_ref, o_ref, *, act):
    acc = _dot(x_ref[...], w_ref[...])
    if act == "sigmoid":
        acc = jax.nn.sigmoid(acc)
    o_ref[...] = acc.astype(o_ref.dtype)


def matmul(x, w, out_dtype, act=None, tm=512, tn=512):
    m, k = x.shape
    n = w.shape[1]
    tm = min(tm, m)
    tn = min(tn, n)
    assert m % tm == 0 and n % tn == 0, (m, n, tm, tn)
    return pl.pallas_call(
        functools.partial(_matmul_kernel, act=act),
        grid=(m // tm, n // tn),
        in_specs=[pl.BlockSpec((tm, k), lambda i, j: (i, 0)),
                  pl.BlockSpec((k, tn), lambda i, j: (0, j))],
        out_specs=pl.BlockSpec((tm, tn), lambda i, j: (i, j)),
        out_shape=jax.ShapeDtypeStruct((m, n), out_dtype),
        compiler_params=_params("parallel", "parallel"),
        name="matmul",
    )(x, w)


def _layer_norm_rows(z, g, b):
    mu = jnp.mean(z, axis=-1, keepdims=True)
    zc = z - mu
    var = jnp.mean(zc * zc, axis=-1, keepdims=True)
    return zc * lax.rsqrt(var + LN_EPS) * g + b


def _ln_res_kernel(h_ref, y_ref, g_ref, b_ref, o_ref, ob_ref, *, alpha, y_transposed):
    y = y_ref[...]
    if y_transposed:
        y = y.T
    out = _layer_norm_rows(alpha * h_ref[...] + y, g_ref[...], b_ref[...])
    o_ref[...] = out
    ob_ref[...] = out.astype(BF16)


def ln_residual(h, y, g, b, alpha, y_transposed=False, tm=256):
    n, d = h.shape
    tm = min(tm, n)
    y_spec = (pl.BlockSpec((d, tm), lambda i: (0, i)) if y_transposed
              else pl.BlockSpec((tm, d), lambda i: (i, 0)))
    return pl.pallas_call(
        functools.partial(_ln_res_kernel, alpha=alpha, y_transposed=y_transposed),
        grid=(n // tm,),
        in_specs=[pl.BlockSpec((tm, d), lambda i: (i, 0)), y_spec,
                  pl.BlockSpec((1, d), lambda i: (0, 0)),
                  pl.BlockSpec((1, d), lambda i: (0, 0))],
        out_specs=[pl.BlockSpec((tm, d), lambda i: (i, 0)),
                   pl.BlockSpec((tm, d), lambda i: (i, 0))],
        out_shape=[jax.ShapeDtypeStruct((n, d), F32), jax.ShapeDtypeStruct((n, d), BF16)],
        compiler_params=_params("parallel"),
        name="ln_residual",
    )(h, y, g.reshape(1, d), b.reshape(1, d))


def _lane_tile(x, width):
    reps = width // LANES
    return x if reps == 1 else jnp.concatenate([x] * reps, axis=1)


def _flash_heads(score_fns, v, m_ref, l_ref, acc_ref):
    n = len(score_fns)
    dv = acc_ref.shape[-1]
    s_next = score_fns[0]()
    pending = None
    for r in range(n):
        s = s_next
        if r + 1 < n:
            s_next = score_fns[r + 1]()
        m_prev = m_ref[r]
        m_next = jnp.maximum(m_prev, jnp.max(s, axis=-1, keepdims=True))
        p = jnp.exp(s - _lane_tile(m_next, s.shape[1]))
        alpha = jnp.exp(m_prev - m_next)
        l_ref[r] = alpha * l_ref[r] + jnp.sum(p, axis=-1, keepdims=True)
        m_ref[r] = m_next
        if pending is not None:
            pr, pa, ppv = pending
            acc_ref[pr] = _lane_tile(pa, dv) * acc_ref[pr] + ppv
        pending = (r, alpha, _dot(p.astype(BF16), v))
    pr, pa, ppv = pending
    acc_ref[pr] = _lane_tile(pa, dv) * acc_ref[pr] + ppv


def _diff_attn_kernel(lam_ref, q_ref, k_ref, v_ref, g_ref, o_ref, m_ref, l_ref, acc_ref,
                      *, tq, tk, lam_init):
    qi = pl.program_id(2)
    scale = LANES ** -0.5
    m_ref[...] = jnp.full(m_ref.shape, NEG_INF, F32)
    l_ref[...] = jnp.zeros(l_ref.shape, F32)
    acc_ref[...] = jnp.zeros(acc_ref.shape, F32)

    def chunk(c, bias):
        k0 = pl.multiple_of(c * tk, tk)
        v = v_ref[pl.ds(k0, tk), :]

        def scores(half):
            sl = slice(half * LANES, (half + 1) * LANES)
            s = _dot(q_ref[:, sl], k_ref[pl.ds(k0, tk), sl], _NT) * scale
            return s if bias is None else s + bias

        _flash_heads([functools.partial(scores, half) for half in range(2)],
                     v, m_ref, l_ref, acc_ref)

    def body(c, carry):
        chunk(c, None)
        return carry

    lax.fori_loop(0, qi, body, 0)
    causal = (lax.broadcasted_iota(I32, (1, tk), 1) <= lax.broadcasted_iota(I32, (tq, 1), 0))
    chunk(qi, jnp.where(causal, 0.0, NEG_INF))

    lv = lam_ref[...]
    lam = (jnp.exp(jnp.sum(lv[0:1] * lv[1:2], axis=-1, keepdims=True))
           - jnp.exp(jnp.sum(lv[2:3] * lv[3:4], axis=-1, keepdims=True)) + lam_init)
    dv = acc_ref.shape[-1]
    o = (acc_ref[0] / _lane_tile(l_ref[0], dv) - lam * (acc_ref[1] / _lane_tile(l_ref[1], dv)))
    o = o * lax.rsqrt(jnp.mean(o * o, axis=-1, keepdims=True) + RMS_EPS)
    o_ref[...] = (o * g_ref[...] * (1.0 - lam_init)).astype(o_ref.dtype)


def diff_attention(z, lam_vec, norm_g, batch, seq, lam_init, tq=512):
    dv = 2 * LANES
    tq = tk = min(tq, seq)
    nq = seq // tq
    return pl.pallas_call(
        functools.partial(_diff_attn_kernel, tq=tq, tk=tk, lam_init=lam_init),
        grid=(batch, DIFF_HEADS, nq),
        in_specs=[pl.BlockSpec((4, LANES), lambda b, h, i: (0, 0)),
                  pl.BlockSpec((tq, dv), lambda b, h, i: (b * nq + i, h)),
                  pl.BlockSpec((seq, dv), lambda b, h, i: (b, DIFF_HEADS + h)),
                  pl.BlockSpec((seq, dv), lambda b, h, i: (b, 2 * DIFF_HEADS + h)),
                  pl.BlockSpec((1, dv), lambda b, h, i: (0, 0))],
        out_specs=pl.BlockSpec((tq, dv), lambda b, h, i: (b * nq + i, h)),
        out_shape=jax.ShapeDtypeStruct((batch * seq, DIFF_HEADS * dv), BF16),
        scratch_shapes=[pltpu.VMEM((2, tq, LANES), F32), pltpu.VMEM((2, tq, LANES), F32),
                        pltpu.VMEM((2, tq, dv), F32)],
        compiler_params=_params("parallel", "parallel", "arbitrary"),
        name="diff_attention",
    )(lam_vec, z, z, z, norm_g.reshape(1, dv))


def _hgrn_kernel(gam_ref, q_ref, f_ref, i_ref, gate_ref, ng_ref, o_ref,
                 st_ref, b_ref, qh_ref, kin_ref, *, tt, layer, heads):
    c_len = HGRN_CHUNK
    width = heads * LANES

    @pl.when(pl.program_id(2) == 0)
    def _():
        st_ref[...] = jnp.zeros(st_ref.shape, F32)

    gam = gam_ref[...]
    e = jnp.exp(gam - jnp.max(gam, axis=0, keepdims=True))
    p = e / jnp.sum(e, axis=0, keepdims=True)
    lb = jnp.zeros((1, width), F32)
    for r in range(1, layer + 1):
        lb = lb + p[r:r + 1]
    log_lb = jnp.log(jnp.maximum(lb, LB_FLOOR))
    z = f_ref[...]
    log_sig = jnp.minimum(z, 0.0) - jnp.log1p(jnp.exp(-jnp.abs(z)))
    log_gate = jnp.log1p(-lb) + log_sig
    log_f = (jnp.maximum(log_lb, log_gate)
             + jnp.log1p(jnp.exp(-jnp.abs(log_lb - log_gate))))
    kin_ref[...] = (1.0 - lb) * jax.nn.sigmoid(-z)
    qh_ref[...] = jax.nn.silu(q_ref[...])
    row = lax.broadcasted_iota(I32, (tt, 1), 0) % c_len
    b = log_f
    step = 1
    while step < c_len:
        b = b + jnp.where(row >= step, pltpu.roll(b, step, axis=0), 0.0)
        step *= 2
    b_ref[...] = b
    trow = lax.broadcasted_iota(I32, (c_len, 1), 0)
    ng = ng_ref[...]

    def body(c, carry):
        rows = pl.ds(pl.multiple_of(c * c_len, c_len), c_len)
        for hd in range(heads):
            cols = slice(hd * LANES, (hd + 1) * LANES)
            bc = b_ref[rows, cols]
            qc = qh_ref[rows, cols]
            kc = kin_ref[rows, cols]
            ic = i_ref[rows, cols]
            st = st_ref[hd]
            o = _dot((qc * jnp.exp(bc)).astype(BF16), st.astype(BF16), _NT)
            for s in range(c_len):
                es = jnp.where(trow >= s, jnp.exp(bc - bc[s:s + 1]), 0.0)
                col = jnp.sum(qc * es * kc[s:s + 1], axis=-1, keepdims=True)
                o = o + col * ic[s:s + 1]
            b_last = bc[c_len - 1:c_len]
            kd = kc * jnp.exp(b_last - bc)
            st_ref[hd] = st * jnp.exp(b_last) + _dot(ic.astype(BF16), kd.astype(BF16), _TN)
            y = o * lax.rsqrt(jnp.mean(o * o, axis=-1, keepdims=True) + RMS_EPS) * ng
            gate = jax.nn.silu(gate_ref[rows, cols])
            o_ref[rows, cols] = (y * gate).astype(o_ref.dtype)
        return carry

    lax.fori_loop(0, tt // c_len, body, 0)


def hgrn2(z, gamma, norm_g, batch, seq, layer, tt=512, heads=4):
    tt = min(tt, seq)
    nt = seq // tt
    n_even = gamma.shape[0]
    ng = HGRN_HEADS // heads
    width = heads * LANES

    def col(offset):
        return pl.BlockSpec((tt, width), lambda b, h, t: (b * nt + t, offset * ng + h))

    tile = pltpu.VMEM((tt, width), F32)
    return pl.pallas_call(
        functools.partial(_hgrn_kernel, tt=tt, layer=layer, heads=heads),
        grid=(batch, ng, nt),
        in_specs=[pl.BlockSpec((n_even, width), lambda b, h, t: (0, h)),
                  col(0), col(1), col(2), col(3),
                  pl.BlockSpec((1, LANES), lambda b, h, t: (0, 0))],
        out_specs=pl.BlockSpec((tt, width), lambda b, h, t: (b * nt + t, h)),
        out_shape=jax.ShapeDtypeStruct((batch * seq, HGRN_HEADS * LANES), BF16),
        scratch_shapes=[pltpu.VMEM((heads, LANES, LANES), F32), tile, tile, tile],
        compiler_params=_params("parallel", "parallel", "arbitrary"),
        name="hgrn2",
    )(gamma, z, z, z, z, norm_g.reshape(1, LANES))


def _topk_rows(vals, k, ids=None):
    r, n = vals.shape
    iota0 = lax.broadcasted_iota(I32, (r, n), 0) if ids is None else ids
    iota_k = lax.broadcasted_iota(I32, (k, n), 0)
    rank = jnp.full((r, n), k, I32)
    tops = []
    stacked = jnp.zeros((k, n), F32)
    for a in range(k):
        m = jnp.max(vals, axis=0, keepdims=True)
        idx = jnp.min(jnp.where(vals == m, iota0, jnp.iinfo(jnp.int32).max), axis=0, keepdims=True)
        hit = iota0 == idx
        rank = jnp.where(hit, a, rank)
        tops.append(m)
        stacked = jnp.where(iota_k == a, m, stacked)
        vals = jnp.where(hit, -jnp.inf, vals)
    return tops, stacked, rank


def _compress_kernel(x_ref, w_ref, pe_ref, o_ref):
    pq = _dot(x_ref[...], w_ref[...])
    nrow = pq.shape[0]
    pe_c = _dot(pe_ref[...], w_ref[...])
    const = pe_c[0:1, :LANES] + pe_c[1:2, LANES:]
    nxt = pltpu.roll(pq[:, LANES:], nrow - 1, axis=0)
    o_ref[...] = (pq[:, :LANES] + nxt + const).astype(o_ref.dtype)


def nsa_compress(xc, wcat, pecat):
    bsz, nj, nrow, kk = xc.shape
    return pl.pallas_call(
        _compress_kernel,
        grid=(bsz, nj),
        in_specs=[pl.BlockSpec((None, None, nrow, kk), lambda b, j: (b, j, 0, 0)),
                  pl.BlockSpec((None, kk, 2 * LANES), lambda b, j: (j // NSA_GROUPS, 0, 0)),
                  pl.BlockSpec((None, 8, kk), lambda b, j: (j // NSA_GROUPS, 0, 0))],
        out_specs=pl.BlockSpec((None, None, nrow, LANES), lambda b, j: (b, j, 0, 0)),
        out_shape=jax.ShapeDtypeStruct((bsz, nj, nrow, LANES), BF16),
        compiler_params=_params("parallel", "parallel"),
        name="nsa_compress",
    )(xc, wcat, pecat)


def _nsa_kernel(q_ref, kc_ref, vc_ref, ks_ref, vs_ref, kw_ref, vw_ref, gt_ref, o_ref,
                m_ref, l_ref, acc_ref, oc_ref, *, tq, tk, ncp, nsp, n_top):
    rep = NSA_REP
    q0 = pl.program_id(2) * tq
    scale = LANES ** -0.5
    qpos = q0 + lax.broadcasted_iota(I32, (tq, 1), 0)

    def q_head(r):
        return q_ref[:, r * LANES:(r + 1) * LANES]

    cend = lax.broadcasted_iota(I32, (1, ncp), 1) * CMP_STRIDE + (CMP_LEN - 1)
    mask_c = cend <= qpos
    bias_c = jnp.where(mask_c, 0.0, NEG_INF)
    psum = jnp.zeros((tq, ncp), F32)
    s_next = _dot(q_head(0), kc_ref[...], _NT)
    for r in range(rep):
        s = s_next * scale + bias_c
        if r + 1 < rep:
            s_next = _dot(q_head(r + 1), kc_ref[...], _NT)
        e = jnp.where(mask_c, jnp.exp(s - jnp.max(s, axis=-1, keepdims=True)), 0.0)
        p = e * (1.0 / jnp.maximum(jnp.sum(e, axis=-1, keepdims=True), 1e-30))
        psum = psum + p
        oc_ref[r] = _dot(p.astype(BF16), vc_ref[...])

    n_i = lax.broadcasted_iota(I32, (ncp, nsp), 0)
    j4 = lax.broadcasted_iota(I32, (ncp, nsp), 1) * (SLC_LEN // CMP_STRIDE)
    overlap = ((n_i >= j4 - (CMP_LEN - 1) // CMP_STRIDE)
               & (n_i < j4 + SLC_LEN // CMP_STRIDE)).astype(F32)
    imp_t = _dot(psum, overlap, precision=lax.Precision.HIGHEST).T
    jt = lax.broadcasted_iota(I32, (nsp, 1), 0)
    cur = (q0 + lax.broadcasted_iota(I32, (1, tq), 1)) // SLC_LEN
    forced = (jt == 0) | (jt == cur) | (jt == cur - 1)
    imp_t = jnp.where(forced, FORCE_SCORE, jnp.where(jt <= cur, imp_t, -1.0))
    _, _, rank = _topk_rows(imp_t, n_top)
    sel = (rank < n_top).astype(F32).T.astype(BF16)

    m_ref[...] = jnp.full(m_ref.shape, NEG_INF, F32)
    l_ref[...] = jnp.zeros(l_ref.shape, F32)
    acc_ref[...] = jnp.zeros(acc_ref.shape, F32)
    jj = lax.broadcasted_iota(I32, (nsp, 1), 0)

    def body(c, carry):
        k0 = pl.multiple_of(c * tk, tk)
        k = ks_ref[pl.ds(k0, tk), :]
        v = vs_ref[pl.ds(k0, tk), :]
        kpos = k0 + lax.broadcasted_iota(I32, (1, tk), 1)
        expand = (jj == kpos // SLC_LEN).astype(BF16)
        allowed = (_dot(sel, expand) > 0.5) & (kpos <= qpos)
        bias = jnp.where(allowed, 0.0, NEG_INF)
        _flash_heads([functools.partial(lambda r: _dot(q_head(r), k, _NT) * scale + bias, r)
                      for r in range(rep)], v, m_ref, l_ref, acc_ref)
        return carry

    lax.fori_loop(0, (q0 + tq + tk - 1) // tk, body, 0)

    span = WINDOW + tq
    start = pl.multiple_of(jnp.maximum(q0 - WINDOW, 0), tq)
    kpos = start + lax.broadcasted_iota(I32, (1, span), 1)
    bias_w = jnp.where((kpos <= qpos) & (kpos > qpos - WINDOW), 0.0, NEG_INF)
    k_w = kw_ref[pl.ds(start, span), :]
    v_w = vw_ref[pl.ds(start, span), :]
    gt = gt_ref[...]
    s_next = _dot(q_head(0), k_w, _NT)
    for r in range(rep):
        s = s_next * scale + bias_w
        if r + 1 < rep:
            s_next = _dot(q_head(r + 1), k_w, _NT)
        e = jnp.exp(s - jnp.max(s, axis=-1, keepdims=True))
        o_w = _dot(e.astype(BF16), v_w) * (1.0 / jnp.sum(e, axis=-1, keepdims=True))
        o_s = acc_ref[r] / l_ref[r]
        o = (gt[:, 3 * r:3 * r + 1] * oc_ref[r] + gt[:, 3 * r + 1:3 * r + 2] * o_s
             + gt[:, 3 * r + 2:3 * r + 3] * o_w)
        o_ref[:, r * LANES:(r + 1) * LANES] = o.astype(o_ref.dtype)


def nsa_attention(z, kvc, gates, batch, seq, tq=128, tk=512):
    tk = min(tk, seq)
    assert seq % tq == 0 and seq >= WINDOW + tq and tq % SLC_LEN == 0 and seq % tk == 0
    nq = seq // tq
    ncp = seq // CMP_STRIDE
    n_slc = seq // SLC_LEN
    nsp = -(-n_slc // LANES) * LANES
    n_top = min(SLC_TOPN, n_slc)
    qw = NSA_REP * LANES
    g = NSA_GROUPS
    q_cols = g * NSA_REP

    def kv_col(which):
        return pl.BlockSpec((seq, LANES), lambda b, gi, i: (b, q_cols + which * g + gi))

    head_state = pltpu.VMEM((NSA_REP, tq, LANES), F32)
    return pl.pallas_call(
        functools.partial(_nsa_kernel, tq=tq, tk=tk, ncp=ncp, nsp=nsp, n_top=n_top),
        grid=(batch, g, nq),
        in_specs=[pl.BlockSpec((tq, qw), lambda b, gi, i: (b * nq + i, gi)),
                  pl.BlockSpec((None, None, ncp, LANES), lambda b, gi, i: (b, gi, 0, 0)),
                  pl.BlockSpec((None, None, ncp, LANES), lambda b, gi, i: (b, g + gi, 0, 0)),
                  kv_col(2), kv_col(3), kv_col(4), kv_col(5),
                  pl.BlockSpec((tq, LANES), lambda b, gi, i: (b * nq + i, gi))],
        out_specs=pl.BlockSpec((tq, qw), lambda b, gi, i: (b * nq + i, gi)),
        out_shape=jax.ShapeDtypeStruct((batch * seq, g * qw), BF16),
        scratch_shapes=[head_state, head_state, head_state, head_state],
        compiler_params=_params("parallel", "parallel", "arbitrary"),
        name="nsa_attention",
    )(z, kvc, kvc, z, z, z, z, gates)


def _xattn_kernel(hb_ref, h_ref, wq_ref, kv_ref, wo_ref, g_ref, b_ref, o_ref, ob_ref, *, alpha):
    width = XATTN_HEADS * LANES
    scale = LANES ** -0.5
    q = _dot(hb_ref[...], wq_ref[...]).astype(BF16)
    kv = kv_ref[...]
    outs = []
    for hd in range(XATTN_HEADS):
        sl = slice(hd * LANES, (hd + 1) * LANES)
        s = _dot(q[:, sl], kv[:, sl], _NT) * scale
        e = jnp.exp(s - jnp.max(s, axis=-1, keepdims=True))
        p = e / jnp.sum(e, axis=-1, keepdims=True)
        outs.append(_dot(p.astype(BF16), kv[:, width + hd * LANES:width + (hd + 1) * LANES]))
    o = jnp.concatenate(outs, axis=1).astype(BF16)
    y = _dot(o, wo_ref[...])
    out = _layer_norm_rows(alpha * h_ref[...] + y, g_ref[...], b_ref[...])
    o_ref[...] = out
    ob_ref[...] = out.astype(BF16)


def xattn_block(hb, h, wq, kv, wo, g, b, alpha, batch, seq, tm=256):
    n, d = h.shape
    tm = min(tm, seq)
    nt = seq // tm
    mem_len = kv.shape[0] // batch
    width = XATTN_HEADS * LANES
    return pl.pallas_call(
        functools.partial(_xattn_kernel, alpha=alpha),
        grid=(batch, nt),
        in_specs=[pl.BlockSpec((tm, d), lambda bi, i: (bi * nt + i, 0)),
                  pl.BlockSpec((tm, d), lambda bi, i: (bi * nt + i, 0)),
                  pl.BlockSpec((d, width), lambda bi, i: (0, 0)),
                  pl.BlockSpec((mem_len, 2 * width), lambda bi, i: (bi, 0)),
                  pl.BlockSpec((width, d), lambda bi, i: (0, 0)),
                  pl.BlockSpec((1, d), lambda bi, i: (0, 0)),
                  pl.BlockSpec((1, d), lambda bi, i: (0, 0))],
        out_specs=[pl.BlockSpec((tm, d), lambda bi, i: (bi * nt + i, 0)),
                   pl.BlockSpec((tm, d), lambda bi, i: (bi * nt + i, 0))],
        out_shape=[jax.ShapeDtypeStruct((n, d), F32), jax.ShapeDtypeStruct((n, d), BF16)],
        compiler_params=_params("parallel", "parallel"),
        name="xattn_block",
    )(hb, h, wq, kv, wo, g.reshape(1, d), b.reshape(1, d))


def _peer_route_kernel(q_ref, keys_ref, mb_ref, e1_ref, b2_ref, e2_ref, *, tn):
    k = PEER_TOPK
    nk = PEER_N_KEYS
    hi = lax.Precision.HIGHEST
    for h in range(PEER_HEADS):
        s1 = _dot(keys_ref[h, 0], q_ref[:, (2 * h) * LANES:(2 * h + 1) * LANES], _NT, hi)
        s2 = _dot(keys_ref[h, 1], q_ref[:, (2 * h + 1) * LANES:(2 * h + 2) * LANES], _NT, hi)
        v1, v1_all, rank1 = _topk_rows(s1, k)
        v2, v2_all, rank2 = _topk_rows(s2, k)
        half = k // 2
        iota_a = lax.broadcasted_iota(I32, (k, tn), 0)
        is_hi = iota_a >= half
        cand = [v1_all + v2[b] for b in range(half)] + [jnp.where(is_hi, v1[0] + v2_all, -jnp.inf)]
        ids = [iota_a * k + b for b in range(half)] + [jnp.where(is_hi, iota_a, iota_a + k * k)]
        tops, _, crank = _topk_rows(jnp.concatenate(cand, axis=0), k, jnp.concatenate(ids, axis=0))
        zsum = tops[0] * 0.0
        for a in range(k):
            zsum = zsum + jnp.exp(tops[a] - tops[0])
        ma = jnp.zeros((k, tn), F32)
        for b in range(half):
            ma = ma + jnp.where(crank[b * k:(b + 1) * k] < k, float(1 << b), 0.0)
        bit = jnp.left_shift(1, iota_a).astype(F32)
        won_hi = (crank[half * k:(half + 1) * k] < k) & is_hi
        extra = jnp.sum(jnp.where(won_hi, bit, 0.0), axis=0, keepdims=True)
        ma_all = (ma + jnp.where(iota_a == 0, extra, 0.0)).astype(I32)
        mb = jnp.zeros((nk, tn), I32)
        b2 = jnp.zeros((nk, tn), I32)
        for a in range(k):
            mb = jnp.where(rank1 == a, ma_all[a:a + 1], mb)
            b2 = jnp.where(rank2 == a, 1 << a, b2)
        mb_ref[h] = mb
        b2_ref[h] = b2
        e1_ref[h] = jnp.where(rank1 < k, jnp.exp(s1 - v1[0]), 0.0) / zsum
        e2_ref[h] = jnp.where(rank2 < k, jnp.exp(s2 - v2[0]), 0.0)


def peer_route(q, keys, tn=256):
    n = q.shape[0]
    tn = min(tn, n)
    tab = lambda dt: jax.ShapeDtypeStruct((PEER_HEADS, PEER_N_KEYS, n), dt)
    tspec = pl.BlockSpec((PEER_HEADS, PEER_N_KEYS, tn), lambda i: (0, 0, i))
    return pl.pallas_call(
        functools.partial(_peer_route_kernel, tn=tn),
        grid=(n // tn,),
        in_specs=[pl.BlockSpec((tn, q.shape[1]), lambda i: (i, 0)),
                  pl.BlockSpec(keys.shape, lambda i: (0, 0, 0, 0))],
        out_specs=[tspec, tspec, tspec, tspec],
        out_shape=[tab(I32), tab(F32), tab(I32), tab(F32)],
        compiler_params=_params("parallel"),
        name="peer_route",
    )(q, keys)


def _peer_dense_kernel(x_ref, u_ref, vt_ref, mb_ref, e1_ref, b2_ref, e2_ref, o_ref, w_ref,
                       *, te, tn):
    j = pl.program_id(1)

    @pl.when(j == 0)
    def _():
        o_ref[...] = jnp.zeros(o_ref.shape, F32)

    nk = PEER_N_KEYS
    for sub in range(te // nk):
        i1 = j * (te // nk) + sub
        w = jnp.zeros((nk, tn), F32)
        for h in range(PEER_HEADS):
            hit = (mb_ref[h, pl.ds(i1, 1), :] & b2_ref[h]) != 0
            w = w + jnp.where(hit, e1_ref[h, pl.ds(i1, 1), :] * e2_ref[h], 0.0)
        w_ref[sub * nk:(sub + 1) * nk, :] = w

    chains = list(range(0, te, PEER_CHAIN))
    hts = [_dot(u_ref[c0:c0 + PEER_CHAIN, :], x_ref[...], _NT) for c0 in chains]
    for c0, ht in zip(chains, hts):
        gelu = 0.5 * ht * (1.0 + lax.erf(ht * (2.0 ** -0.5)))
        a = (gelu * w_ref[c0:c0 + PEER_CHAIN, :]).astype(BF16)
        o_ref[...] += _dot(vt_ref[:, c0:c0 + PEER_CHAIN], a)


def peer_dense(xb, u, vt, tables, tn=512, te=512):
    n, d = xb.shape
    n_exp = u.shape[0]
    tn = min(tn, n)
    assert te % PEER_CHAIN == 0 and n_exp % te == 0
    once = pl.Buffered(1)
    tspec = pl.BlockSpec((PEER_HEADS, PEER_N_KEYS, tn), lambda i, j: (0, 0, i), pipeline_mode=once)
    return pl.pallas_call(
        functools.partial(_peer_dense_kernel, te=te, tn=tn),
        grid=(n // tn, n_exp // te),
        in_specs=[pl.BlockSpec((tn, d), lambda i, j: (i, 0), pipeline_mode=once),
                  pl.BlockSpec((te, d), lambda i, j: (j, 0)),
                  pl.BlockSpec((d, te), lambda i, j: (0, j)),
                  tspec, tspec, tspec, tspec],
        out_specs=pl.BlockSpec((d, tn), lambda i, j: (0, i)),
        out_shape=jax.ShapeDtypeStruct((d, n), F32),
        scratch_shapes=[pltpu.VMEM((te, tn), F32)],
        compiler_params=_params("parallel", "arbitrary"),
        name="peer_dense",
    )(xb, u, vt, *tables)


def odd_mixer_core(hb, w_in, cmp_pe, cmp_w, batch, seq, nsa_tq=128):
    q_w = NSA_GROUPS * NSA_REP * LANES
    kv_w = NSA_GROUPS * LANES
    nsa_main = q_w + 6 * kv_w
    n_gate = 3 * NSA_GROUPS * NSA_REP
    z = matmul(hb, w_in[:, :nsa_main].astype(BF16), BF16)
    per_group = n_gate // NSA_GROUPS
    w_gate = w_in[:, nsa_main:].reshape(-1, NSA_GROUPS, per_group)
    w_gate = jnp.pad(w_gate, ((0, 0), (0, 0), (0, LANES - per_group)))
    w_gate = w_gate.reshape(-1, NSA_GROUPS * LANES).astype(BF16)
    gates = matmul(hb, w_gate, F32, act="sigmoid")
    xc = z[:, q_w:q_w + 2 * kv_w].reshape(batch, seq // CMP_STRIDE, CMP_STRIDE, 2 * NSA_GROUPS, LANES)
    xc = xc.transpose(0, 3, 1, 2, 4).reshape(
        batch, 2 * NSA_GROUPS, seq // CMP_STRIDE, CMP_STRIDE * LANES)
    half = CMP_STRIDE * LANES
    cw = cmp_w.astype(BF16)
    wcat = jnp.concatenate([cw[:, :half], cw[:, half:]], axis=2)
    pe = cmp_pe.reshape(2, 2, 1, half).astype(BF16)
    pecat = jnp.concatenate([pe[:, 0], pe[:, 1], jnp.zeros((2, 6, half), BF16)], axis=1)
    kvc = nsa_compress(xc, wcat, pecat)
    return nsa_attention(z, kvc, gates, batch, seq, tq=nsa_tq)


def kernel(x, mem, w_in_even, w_out_even, diff_lambda, diff_norm_g, hgrn_gamma, hgrn_norm_g,
           w_in_odd, w_out_odd, cmp_pe, cmp_w, xattn_wq, xattn_wkv, xattn_wo, ln_g, ln_b,
           peer_wq, peer_keys, peer_u, peer_v):
    batch, seq, d = x.shape
    depth = ln_g.shape[0]
    n = batch * seq
    assert d == 32 * LANES, "head dims are d_model/32 and must equal the lane count"
    alpha = (2.0 * depth) ** 0.25
    diff_w = DIFF_HEADS * 2 * LANES
    attn_cols = 3 * diff_w

    h = x.reshape(n, d)
    hb = h.astype(BF16)
    memb = mem.reshape(-1, d).astype(BF16)

    for l in range(depth):
        j = l // 2
        if l % 2 == 0:
            lam_init = 0.8 - 0.6 * math.exp(-0.3 * l)
            w_in = w_in_even[j].astype(BF16)
            z_attn = matmul(hb, w_in[:, :attn_cols], BF16)
            z_rec = matmul(hb, w_in[:, attn_cols:], F32)
            o_a = diff_attention(z_attn, diff_lambda[j], diff_norm_g[j], batch, seq, lam_init)
            o_b = hgrn2(z_rec, hgrn_gamma, hgrn_norm_g[j], batch, seq, j)
            mix = jnp.concatenate([o_a, o_b], axis=1)
            y = matmul(mix, w_out_even[j].astype(BF16), F32)
        else:
            o = odd_mixer_core(hb, w_in_odd[j], cmp_pe[j], cmp_w[j], batch, seq)
            y = matmul(o, w_out_odd[j].astype(BF16), F32)
        h, hb = ln_residual(h, y, ln_g[l, 0], ln_b[l, 0], alpha)

        kv = matmul(memb, xattn_wkv[l].astype(BF16), BF16)
        h, hb = xattn_block(hb, h, xattn_wq[l].astype(BF16), kv, xattn_wo[l].astype(BF16),
                            ln_g[l, 1], ln_b[l, 1], alpha, batch, seq)

        pq = matmul(hb, peer_wq[l].astype(BF16), F32)
        tables = peer_route(pq, peer_keys[l])
        yt = peer_dense(hb, peer_u[l].astype(BF16), peer_v[l].T.astype(BF16), tables)
        h, hb = ln_residual(h, yt, ln_g[l, 2], ln_b[l, 2], alpha, y_transposed=True)

    return h.reshape(batch, seq, d)
```

```python
import functools
import math

import jax
import jax.numpy as jnp
from jax import lax
from jax.experimental import pallas as pl
from jax.experimental.pallas import tpu as pltpu

F32 = jnp.float32
BF16 = jnp.bfloat16
I32 = jnp.int32

LANES = 128
VMEM_LIMIT_BYTES = 56 * 1024 * 1024

DIFF_HEADS = 8
HGRN_HEADS = 16
HGRN_CHUNK = 16
LB_FLOOR = 1e-30
NSA_GROUPS = 4
NSA_REP = 8
CMP_LEN = 32
CMP_STRIDE = 16
SLC_LEN = 64
SLC_TOPN = 16
WINDOW = 512
FORCE_SCORE = 1e9
XATTN_HEADS = 4
PEER_HEADS = 8
PEER_N_KEYS = 128
PEER_TOPK = 16
FLASH_LOOKAHEAD = 3
PEER_CHAIN = 256
QK_LOG2_SCALE = LANES ** -0.5 * math.log2(math.e)
LN_EPS = 1e-5
RMS_EPS = 1e-6
NEG_INF = -1e30

_NT = (((1,), (1,)), ((), ()))
_TN = (((0,), (0,)), ((), ()))


def _params(*sem):
    return pltpu.CompilerParams(dimension_semantics=sem, vmem_limit_bytes=VMEM_LIMIT_BYTES)


def _dot(a, b, dims=None, precision=None):
    if dims is None:
        return jnp.dot(a, b, preferred_element_type=F32, precision=precision)
    return lax.dot_general(a, b, dims, preferred_element_type=F32, precision=precision)


def _matmul_kernel(x_ref, w_ref, o_ref, *, act, scaled_blocks, scale):
    acc = _dot(x_ref[...], w_ref[...])
    if scaled_blocks:
        acc = acc * jnp.where(pl.program_id(1) < scaled_blocks, scale, 1.0)
    if act == "sigmoid":
        acc = jax.nn.sigmoid(acc)
    o_ref[...] = acc.astype(o_ref.dtype)


def matmul(x, w, out_dtype, act=None, scaled_cols=0, scale=1.0, tm=512, tn=512):
    m, k = x.shape
    n = w.shape[1]
    tm = min(tm, m)
    tn = min(tn, n)
    assert m % tm == 0 and n % tn == 0 and scaled_cols % tn == 0, (m, n, tm, tn, scaled_cols)
    return pl.pallas_call(
        functools.partial(_matmul_kernel, act=act, scaled_blocks=scaled_cols // tn, scale=scale),
        grid=(m // tm, n // tn),
        in_specs=[pl.BlockSpec((tm, k), lambda i, j: (i, 0)),
                  pl.BlockSpec((k, tn), lambda i, j: (0, j))],
        out_specs=pl.BlockSpec((tm, tn), lambda i, j: (i, j)),
        out_shape=jax.ShapeDtypeStruct((m, n), out_dtype),
        compiler_params=_params("parallel", "parallel"),
        name="matmul",
    )(x, w)


def _layer_norm_rows(z, g, b):
    mu = jnp.mean(z, axis=-1, keepdims=True)
    zc = z - mu
    var = jnp.mean(zc * zc, axis=-1, keepdims=True)
    return zc * lax.rsqrt(var + LN_EPS) * g + b


def _ln_res_kernel(h_ref, y_ref, g_ref, b_ref, o_ref, ob_ref, *, alpha, y_transposed):
    y = y_ref[...]
    if y_transposed:
        y = y.T
    out = _layer_norm_rows(alpha * h_ref[...] + y, g_ref[...], b_ref[...])
    o_ref[...] = out
    ob_ref[...] = out.astype(BF16)


def ln_residual(h, y, g, b, alpha, y_transposed=False, tm=256):
    n, d = h.shape
    tm = min(tm, n)
    y_spec = (pl.BlockSpec((d, tm), lambda i: (0, i)) if y_transposed
              else pl.BlockSpec((tm, d), lambda i: (i, 0)))
    return pl.pallas_call(
        functools.partial(_ln_res_kernel, alpha=alpha, y_transposed=y_transposed),
        grid=(n // tm,),
        in_specs=[pl.BlockSpec((tm, d), lambda i: (i, 0)), y_spec,
                  pl.BlockSpec((1, d), lambda i: (0, 0)),
                  pl.BlockSpec((1, d), lambda i: (0, 0))],
        out_specs=[pl.BlockSpec((tm, d), lambda i: (i, 0)),
                   pl.BlockSpec((tm, d), lambda i: (i, 0))],
        out_shape=[jax.ShapeDtypeStruct((n, d), F32), jax.ShapeDtypeStruct((n, d), BF16)],
        compiler_params=_params("parallel"),
        name="ln_residual",
    )(h, y, g.reshape(1, d), b.reshape(1, d))


def _lane_tile(x, width):
    reps = width // LANES
    return x if reps == 1 else jnp.concatenate([x] * reps, axis=1)


def _flash_heads(score_fns, v, m_ref, l_ref, acc_ref, psum_ref=None, row_ok=None):
    n = len(score_fns)
    dv = acc_ref.shape[-1]
    ahead = min(FLASH_LOOKAHEAD, n)
    scores = [score_fns[r]() for r in range(ahead)]
    pending = []

    def retire():
        pr, pa, ppv = pending.pop(0)
        acc_ref[pr] = _lane_tile(pa, dv) * acc_ref[pr] + ppv

    for r in range(n):
        s = scores.pop(0)
        if r + ahead < n:
            scores.append(score_fns[r + ahead]())
        m_prev = m_ref[r]
        m_next = jnp.maximum(m_prev, jnp.max(s, axis=-1, keepdims=True))
        p = jnp.exp2(s - _lane_tile(m_next, s.shape[1]))
        alpha = jnp.exp2(m_prev - m_next)
        l_new = alpha * l_ref[r] + jnp.sum(p, axis=-1, keepdims=True)
        l_ref[r] = l_new
        m_ref[r] = m_next
        if psum_ref is not None:
            psum_ref[...] += p * _lane_tile(jnp.where(row_ok, 1.0 / l_new, 0.0), s.shape[1])
        if len(pending) == ahead:
            retire()
        pending.append((r, alpha, _dot(p.astype(BF16), v)))
    while pending:
        retire()


def _diff_attn_kernel(lam_ref, q_ref, k_ref, v_ref, g_ref, o_ref, m_ref, l_ref, acc_ref,
                      *, tq, tk, lam_init):
    qi = pl.program_id(2)
    m_ref[...] = jnp.full(m_ref.shape, NEG_INF, F32)
    l_ref[...] = jnp.zeros(l_ref.shape, F32)
    acc_ref[...] = jnp.zeros(acc_ref.shape, F32)

    def chunk(c, bias):
        k0 = pl.multiple_of(c * tk, tk)
        v = v_ref[pl.ds(k0, tk), :]

        def scores(half):
            sl = slice(half * LANES, (half + 1) * LANES)
            s = _dot(q_ref[:, sl], k_ref[pl.ds(k0, tk), sl], _NT)
            return s if bias is None else s + bias

        _flash_heads([functools.partial(scores, half) for half in range(2)],
                     v, m_ref, l_ref, acc_ref)

    def body(c, carry):
        chunk(c, None)
        return carry

    lax.fori_loop(0, qi, body, 0)
    causal = (lax.broadcasted_iota(I32, (1, tk), 1) <= lax.broadcasted_iota(I32, (tq, 1), 0))
    chunk(qi, jnp.where(causal, 0.0, NEG_INF))

    lv = lam_ref[...]
    lam = (jnp.exp(jnp.sum(lv[0:1] * lv[1:2], axis=-1, keepdims=True))
           - jnp.exp(jnp.sum(lv[2:3] * lv[3:4], axis=-1, keepdims=True)) + lam_init)
    dv = acc_ref.shape[-1]
    o = (acc_ref[0] / _lane_tile(l_ref[0], dv) - lam * (acc_ref[1] / _lane_tile(l_ref[1], dv)))
    o = o * lax.rsqrt(jnp.mean(o * o, axis=-1, keepdims=True) + RMS_EPS)
    o_ref[...] = (o * g_ref[...] * (1.0 - lam_init)).astype(o_ref.dtype)


def diff_attention(z, lam_vec, norm_g, batch, seq, lam_init, tq=512):
    dv = 2 * LANES
    tq = tk = min(tq, seq)
    nq = seq // tq
    return pl.pallas_call(
        functools.partial(_diff_attn_kernel, tq=tq, tk=tk, lam_init=lam_init),
        grid=(batch, DIFF_HEADS, nq),
        in_specs=[pl.BlockSpec((4, LANES), lambda b, h, i: (0, 0)),
                  pl.BlockSpec((tq, dv), lambda b, h, i: (b * nq + i, h)),
                  pl.BlockSpec((seq, dv), lambda b, h, i: (b, DIFF_HEADS + h)),
                  pl.BlockSpec((seq, dv), lambda b, h, i: (b, 2 * DIFF_HEADS + h)),
                  pl.BlockSpec((1, dv), lambda b, h, i: (0, 0))],
        out_specs=pl.BlockSpec((tq, dv), lambda b, h, i: (b * nq + i, h)),
        out_shape=jax.ShapeDtypeStruct((batch * seq, DIFF_HEADS * dv), BF16),
        scratch_shapes=[pltpu.VMEM((2, tq, LANES), F32), pltpu.VMEM((2, tq, LANES), F32),
                        pltpu.VMEM((2, tq, dv), F32)],
        compiler_params=_params("parallel", "parallel", "arbitrary"),
        name="diff_attention",
    )(lam_vec, z, z, z, norm_g.reshape(1, dv))


def _hgrn_kernel(gam_ref, q_ref, f_ref, i_ref, gate_ref, ng_ref, o_ref,
                 st_ref, b_ref, qh_ref, kin_ref, *, tt, layer, heads):
    c_len = HGRN_CHUNK
    width = heads * LANES

    @pl.when(pl.program_id(2) == 0)
    def _():
        st_ref[...] = jnp.zeros(st_ref.shape, F32)

    gam = gam_ref[...]
    e = jnp.exp(gam - jnp.max(gam, axis=0, keepdims=True))
    p = e / jnp.sum(e, axis=0, keepdims=True)
    lb = jnp.zeros((1, width), F32)
    for r in range(1, layer + 1):
        lb = lb + p[r:r + 1]
    log_lb = jnp.log(jnp.maximum(lb, LB_FLOOR))
    z = f_ref[...]
    log_sig = jnp.minimum(z, 0.0) - jnp.log1p(jnp.exp(-jnp.abs(z)))
    log_gate = jnp.log1p(-lb) + log_sig
    log_f = (jnp.maximum(log_lb, log_gate)
             + jnp.log1p(jnp.exp(-jnp.abs(log_lb - log_gate))))
    kin_ref[...] = (1.0 - lb) * jax.nn.sigmoid(-z)
    qh_ref[...] = jax.nn.silu(q_ref[...])
    row = lax.broadcasted_iota(I32, (tt, 1), 0) % c_len
    b = log_f
    step = 1
    while step < c_len:
        b = b + jnp.where(row >= step, pltpu.roll(b, step, axis=0), 0.0)
        step *= 2
    b_ref[...] = b
    trow = lax.broadcasted_iota(I32, (c_len, 1), 0)
    ng = ng_ref[...]

    def body(c, carry):
        rows = pl.ds(pl.multiple_of(c * c_len, c_len), c_len)
        for hd in range(heads):
            cols = slice(hd * LANES, (hd + 1) * LANES)
            bc = b_ref[rows, cols]
            qc = qh_ref[rows, cols]
            kc = kin_ref[rows, cols]
            ic = i_ref[rows, cols]
            st = st_ref[hd]
            o = _dot((qc * jnp.exp(bc)).astype(BF16), st.astype(BF16), _NT)
            for s in range(c_len):
                es = jnp.where(trow >= s, jnp.exp(bc - bc[s:s + 1]), 0.0)
                col = jnp.sum(qc * es * kc[s:s + 1], axis=-1, keepdims=True)
                o = o + col * ic[s:s + 1]
            b_last = bc[c_len - 1:c_len]
            kd = kc * jnp.exp(b_last - bc)
            st_ref[hd] = st * jnp.exp(b_last) + _dot(ic.astype(BF16), kd.astype(BF16), _TN)
            y = o * lax.rsqrt(jnp.mean(o * o, axis=-1, keepdims=True) + RMS_EPS) * ng
            gate = jax.nn.silu(gate_ref[rows, cols])
            o_ref[rows, cols] = (y * gate).astype(o_ref.dtype)
        return carry

    lax.fori_loop(0, tt // c_len, body, 0)


def hgrn2(z, gamma, norm_g, batch, seq, layer, tt=512, heads=4):
    tt = min(tt, seq)
    nt = seq // tt
    n_even = gamma.shape[0]
    ng = HGRN_HEADS // heads
    width = heads * LANES

    def col(offset):
        return pl.BlockSpec((tt, width), lambda b, h, t: (b * nt + t, offset * ng + h))

    tile = pltpu.VMEM((tt, width), F32)
    return pl.pallas_call(
        functools.partial(_hgrn_kernel, tt=tt, layer=layer, heads=heads),
        grid=(batch, ng, nt),
        in_specs=[pl.BlockSpec((n_even, width), lambda b, h, t: (0, h)),
                  col(0), col(1), col(2), col(3),
                  pl.BlockSpec((1, LANES), lambda b, h, t: (0, 0))],
        out_specs=pl.BlockSpec((tt, width), lambda b, h, t: (b * nt + t, h)),
        out_shape=jax.ShapeDtypeStruct((batch * seq, HGRN_HEADS * LANES), BF16),
        scratch_shapes=[pltpu.VMEM((heads, LANES, LANES), F32), tile, tile, tile],
        compiler_params=_params("parallel", "parallel", "arbitrary"),
        name="hgrn2",
    )(gamma, z, z, z, z, norm_g.reshape(1, LANES))


def _topk_rows(vals, k, ids=None):
    r, n = vals.shape
    iota0 = lax.broadcasted_iota(I32, (r, n), 0) if ids is None else ids
    iota_k = lax.broadcasted_iota(I32, (k, n), 0)
    rank = jnp.full((r, n), k, I32)
    tops = []
    stacked = jnp.zeros((k, n), F32)
    for a in range(k):
        m = jnp.max(vals, axis=0, keepdims=True)
        idx = jnp.min(jnp.where(vals == m, iota0, jnp.iinfo(jnp.int32).max), axis=0, keepdims=True)
        hit = iota0 == idx
        rank = jnp.where(hit, a, rank)
        tops.append(m)
        stacked = jnp.where(iota_k == a, m, stacked)
        vals = jnp.where(hit, -jnp.inf, vals)
    return tops, stacked, rank


def _compress_kernel(x_ref, w_ref, pe_ref, o_ref):
    pq = _dot(x_ref[...], w_ref[...])
    nrow = pq.shape[0]
    pe_c = _dot(pe_ref[...], w_ref[...])
    const = pe_c[0:1, :LANES] + pe_c[1:2, LANES:]
    nxt = pltpu.roll(pq[:, LANES:], nrow - 1, axis=0)
    o_ref[...] = (pq[:, :LANES] + nxt + const).astype(o_ref.dtype)


def nsa_compress(xc, wcat, pecat):
    bsz, nj, nrow, kk = xc.shape
    return pl.pallas_call(
        _compress_kernel,
        grid=(bsz, nj),
        in_specs=[pl.BlockSpec((None, None, nrow, kk), lambda b, j: (b, j, 0, 0)),
                  pl.BlockSpec((None, kk, 2 * LANES), lambda b, j: (j // NSA_GROUPS, 0, 0)),
                  pl.BlockSpec((None, 8, kk), lambda b, j: (j // NSA_GROUPS, 0, 0))],
        out_specs=pl.BlockSpec((None, None, nrow, LANES), lambda b, j: (b, j, 0, 0)),
        out_shape=jax.ShapeDtypeStruct((bsz, nj, nrow, LANES), BF16),
        compiler_params=_params("parallel", "parallel"),
        name="nsa_compress",
    )(xc, wcat, pecat)


def _nsa_kernel(q_ref, kc_ref, vc_ref, ks_ref, vs_ref, kw_ref, vw_ref, gt_ref, o_ref,
                m_ref, l_ref, acc_ref, mw_ref, lw_ref, accw_ref, mc_ref, lc_ref, accc_ref,
                psum_ref, sel_ref, *, tq, tk, ncp, nsp, n_top):
    rep = NSA_REP
    q0 = pl.program_id(2) * tq
    qpos = q0 + lax.broadcasted_iota(I32, (tq, 1), 0)

    def q_head(r):
        return q_ref[:, r * LANES:(r + 1) * LANES]

    for ref in (m_ref, mw_ref, mc_ref):
        ref[...] = jnp.full(ref.shape, NEG_INF, F32)
    for ref in (l_ref, acc_ref, lw_ref, accw_ref, lc_ref, accc_ref):
        ref[...] = jnp.zeros(ref.shape, F32)

    def once(fn):
        lax.fori_loop(0, jnp.minimum(pl.program_id(2), 0) + 1, lambda _, carry: (fn(), carry)[1], 0)

    has_key = qpos >= CMP_LEN - 1

    def compressed():
        cend = lax.broadcasted_iota(I32, (1, ncp), 1) * CMP_STRIDE + (CMP_LEN - 1)
        bias_c = jnp.where(cend <= qpos, 0.0, NEG_INF)
        psum_ref[...] = jnp.zeros(psum_ref.shape, F32)
        kc = kc_ref[...]
        _flash_heads([functools.partial(lambda r: _dot(q_head(r), kc, _NT) + bias_c, r)
                      for r in range(rep)], vc_ref[...], mc_ref, lc_ref, accc_ref, psum_ref, has_key)

    once(compressed)

    n_i = lax.broadcasted_iota(I32, (ncp, nsp), 0)
    j4 = lax.broadcasted_iota(I32, (ncp, nsp), 1) * (SLC_LEN // CMP_STRIDE)
    overlap = ((n_i >= j4 - (CMP_LEN - 1) // CMP_STRIDE)
               & (n_i < j4 + SLC_LEN // CMP_STRIDE)).astype(F32)
    imp_t = _dot(psum_ref[...], overlap, precision=lax.Precision.HIGHEST).T
    jt = lax.broadcasted_iota(I32, (nsp, 1), 0)
    cur = (q0 + lax.broadcasted_iota(I32, (1, tq), 1)) // SLC_LEN
    forced = (jt == 0) | (jt == cur) | (jt == cur - 1)
    imp_t = jnp.where(forced, FORCE_SCORE, jnp.where(jt <= cur, imp_t, -1.0))
    _, _, rank = _topk_rows(imp_t, n_top)
    sel_ref[...] = (rank < n_top).astype(F32).T.astype(BF16)

    jj = lax.broadcasted_iota(I32, (nsp, 1), 0)

    def body(c, carry):
        k0 = pl.multiple_of(c * tk, tk)
        k = ks_ref[pl.ds(k0, tk), :]
        v = vs_ref[pl.ds(k0, tk), :]
        kpos = k0 + lax.broadcasted_iota(I32, (1, tk), 1)
        expand = (jj == kpos // SLC_LEN).astype(BF16)
        allowed = (_dot(sel_ref[...], expand) > 0.5) & (kpos <= qpos)
        bias = jnp.where(allowed, 0.0, NEG_INF)
        _flash_heads([functools.partial(lambda r: _dot(q_head(r), k, _NT) + bias, r)
                      for r in range(rep)], v, m_ref, l_ref, acc_ref)
        return carry

    lax.fori_loop(0, (q0 + tq + tk - 1) // tk, body, 0)

    span = WINDOW + tq
    start = pl.multiple_of(jnp.maximum(q0 - WINDOW, 0), tq)

    def window():
        kpos = start + lax.broadcasted_iota(I32, (1, span), 1)
        bias_w = jnp.where((kpos <= qpos) & (kpos > qpos - WINDOW), 0.0, NEG_INF)
        k_w = kw_ref[pl.ds(start, span), :]
        _flash_heads([functools.partial(lambda r: _dot(q_head(r), k_w, _NT) + bias_w, r)
                      for r in range(rep)], vw_ref[pl.ds(start, span), :], mw_ref, lw_ref, accw_ref)

    once(window)

    gt = gt_ref[...]
    for r in range(rep):
        o = (gt[:, 3 * r:3 * r + 1] * jnp.where(has_key, accc_ref[r] / lc_ref[r], 0.0)
             + gt[:, 3 * r + 1:3 * r + 2] * (acc_ref[r] / l_ref[r])
             + gt[:, 3 * r + 2:3 * r + 3] * (accw_ref[r] / lw_ref[r]))
        o_ref[:, r * LANES:(r + 1) * LANES] = o.astype(o_ref.dtype)


def nsa_attention(z, kvc, gates, batch, seq, tq=128, tk=512):
    tk = min(tk, seq)
    assert seq % tq == 0 and seq >= WINDOW + tq and tq % SLC_LEN == 0 and seq % tk == 0
    nq = seq // tq
    ncp = seq // CMP_STRIDE
    n_slc = seq // SLC_LEN
    nsp = -(-n_slc // LANES) * LANES
    n_top = min(SLC_TOPN, n_slc)
    qw = NSA_REP * LANES
    g = NSA_GROUPS
    q_cols = g * NSA_REP

    def kv_col(which):
        return pl.BlockSpec((seq, LANES), lambda b, gi, i: (b, q_cols + which * g + gi))

    head_state = pltpu.VMEM((NSA_REP, tq, LANES), F32)
    return pl.pallas_call(
        functools.partial(_nsa_kernel, tq=tq, tk=tk, ncp=ncp, nsp=nsp, n_top=n_top),
        grid=(batch, g, nq),
        in_specs=[pl.BlockSpec((tq, qw), lambda b, gi, i: (b * nq + i, gi)),
                  pl.BlockSpec((None, None, ncp, LANES), lambda b, gi, i: (b, gi, 0, 0)),
                  pl.BlockSpec((None, None, ncp, LANES), lambda b, gi, i: (b, g + gi, 0, 0)),
                  kv_col(2), kv_col(3), kv_col(4), kv_col(5),
                  pl.BlockSpec((tq, LANES), lambda b, gi, i: (b * nq + i, gi))],
        out_specs=pl.BlockSpec((tq, qw), lambda b, gi, i: (b * nq + i, gi)),
        out_shape=jax.ShapeDtypeStruct((batch * seq, g * qw), BF16),
        scratch_shapes=[head_state] * 9 + [pltpu.VMEM((tq, ncp), F32), pltpu.VMEM((tq, nsp), BF16)],
        compiler_params=_params("parallel", "parallel", "arbitrary"),
        name="nsa_attention",
    )(z, kvc, kvc, z, z, z, z, gates)


def _xattn_kernel(hb_ref, h_ref, wq_ref, kv_ref, wo_ref, g_ref, b_ref, o_ref, ob_ref, *, alpha):
    width = XATTN_HEADS * LANES
    scale = LANES ** -0.5
    q = _dot(hb_ref[...], wq_ref[...]).astype(BF16)
    kv = kv_ref[...]
    outs = []
    for hd in range(XATTN_HEADS):
        sl = slice(hd * LANES, (hd + 1) * LANES)
        s = _dot(q[:, sl], kv[:, sl], _NT) * scale
        e = jnp.exp(s - jnp.max(s, axis=-1, keepdims=True))
        p = e / jnp.sum(e, axis=-1, keepdims=True)
        outs.append(_dot(p.astype(BF16), kv[:, width + hd * LANES:width + (hd + 1) * LANES]))
    o = jnp.concatenate(outs, axis=1).astype(BF16)
    y = _dot(o, wo_ref[...])
    out = _layer_norm_rows(alpha * h_ref[...] + y, g_ref[...], b_ref[...])
    o_ref[...] = out
    ob_ref[...] = out.astype(BF16)


def xattn_block(hb, h, wq, kv, wo, g, b, alpha, batch, seq, tm=256):
    n, d = h.shape
    tm = min(tm, seq)
    nt = seq // tm
    mem_len = kv.shape[0] // batch
    width = XATTN_HEADS * LANES
    return pl.pallas_call(
        functools.partial(_xattn_kernel, alpha=alpha),
        grid=(batch, nt),
        in_specs=[pl.BlockSpec((tm, d), lambda bi, i: (bi * nt + i, 0)),
                  pl.BlockSpec((tm, d), lambda bi, i: (bi * nt + i, 0)),
                  pl.BlockSpec((d, width), lambda bi, i: (0, 0)),
                  pl.BlockSpec((mem_len, 2 * width), lambda bi, i: (bi, 0)),
                  pl.BlockSpec((width, d), lambda bi, i: (0, 0)),
                  pl.BlockSpec((1, d), lambda bi, i: (0, 0)),
                  pl.BlockSpec((1, d), lambda bi, i: (0, 0))],
        out_specs=[pl.BlockSpec((tm, d), lambda bi, i: (bi * nt + i, 0)),
                   pl.BlockSpec((tm, d), lambda bi, i: (bi * nt + i, 0))],
        out_shape=[jax.ShapeDtypeStruct((n, d), F32), jax.ShapeDtypeStruct((n, d), BF16)],
        compiler_params=_params("parallel", "parallel"),
        name="xattn_block",
    )(hb, h, wq, kv, wo, g.reshape(1, d), b.reshape(1, d))


def _peer_route_kernel(q_ref, keys_ref, mb_ref, e1_ref, b2_ref, e2_ref, *, tn):
    k = PEER_TOPK
    nk = PEER_N_KEYS
    hi = lax.Precision.HIGHEST
    for h in range(PEER_HEADS):
        s1 = _dot(keys_ref[h, 0], q_ref[:, (2 * h) * LANES:(2 * h + 1) * LANES], _NT, hi)
        s2 = _dot(keys_ref[h, 1], q_ref[:, (2 * h + 1) * LANES:(2 * h + 2) * LANES], _NT, hi)
        v1, v1_all, rank1 = _topk_rows(s1, k)
        v2, v2_all, rank2 = _topk_rows(s2, k)
        half = k // 2
        iota_a = lax.broadcasted_iota(I32, (k, tn), 0)
        is_hi = iota_a >= half
        cand = [v1_all + v2[b] for b in range(half)] + [jnp.where(is_hi, v1[0] + v2_all, -jnp.inf)]
        ids = [iota_a * k + b for b in range(half)] + [jnp.where(is_hi, iota_a, iota_a + k * k)]
        tops, _, crank = _topk_rows(jnp.concatenate(cand, axis=0), k, jnp.concatenate(ids, axis=0))
        zsum = tops[0] * 0.0
        for a in range(k):
            zsum = zsum + jnp.exp(tops[a] - tops[0])
        ma = jnp.zeros((k, tn), F32)
        for b in range(half):
            ma = ma + jnp.where(crank[b * k:(b + 1) * k] < k, float(1 << b), 0.0)
        bit = jnp.left_shift(1, iota_a).astype(F32)
        won_hi = (crank[half * k:(half + 1) * k] < k) & is_hi
        extra = jnp.sum(jnp.where(won_hi, bit, 0.0), axis=0, keepdims=True)
        ma_all = (ma + jnp.where(iota_a == 0, extra, 0.0)).astype(I32)
        mb = jnp.zeros((nk, tn), I32)
        b2 = jnp.zeros((nk, tn), I32)
        for a in range(k):
            mb = jnp.where(rank1 == a, ma_all[a:a + 1], mb)
            b2 = jnp.where(rank2 == a, 1 << a, b2)
        mb_ref[h] = jnp.where(mb >= 1 << (k - 1), mb - (1 << k), mb)
        b2_ref[h] = jnp.where(b2 >= 1 << (k - 1), b2 - (1 << k), b2).astype(b2_ref.dtype)
        e1_ref[h] = jnp.where(rank1 < k, jnp.exp(s1 - v1[0]), 0.0) / zsum
        e2_ref[h] = jnp.where(rank2 < k, jnp.exp(s2 - v2[0]), 0.0).astype(e2_ref.dtype)


def peer_route(q, keys, tn=256):
    n = q.shape[0]
    tn = min(tn, n)
    tab = lambda dt: jax.ShapeDtypeStruct((PEER_HEADS, PEER_N_KEYS, n), dt)
    tspec = pl.BlockSpec((PEER_HEADS, PEER_N_KEYS, tn), lambda i: (0, 0, i))
    return pl.pallas_call(
        functools.partial(_peer_route_kernel, tn=tn),
        grid=(n // tn,),
        in_specs=[pl.BlockSpec((tn, q.shape[1]), lambda i: (i, 0)),
                  pl.BlockSpec(keys.shape, lambda i: (0, 0, 0, 0))],
        out_specs=[tspec, tspec, tspec, tspec],
        out_shape=[tab(I32), tab(F32), tab(jnp.int16), tab(BF16)],
        compiler_params=_params("parallel"),
        name="peer_route",
    )(q, keys)


def _peer_dense_kernel(x_ref, u_ref, vt_ref, mb_ref, e1_ref, b2_ref, e2_ref, o_ref, w_ref,
                       *, te, tn):
    j = pl.program_id(1)

    @pl.when(j == 0)
    def _():
        o_ref[...] = jnp.zeros(o_ref.shape, F32)

    nk = PEER_N_KEYS
    for sub in range(te // nk):
        i1 = j * (te // nk) + sub
        w = jnp.zeros((nk, tn), BF16)
        for h in range(PEER_HEADS):
            hit = (mb_ref[h, pl.ds(i1, 1), :].astype(jnp.int16) & b2_ref[h]) != 0
            gate = e1_ref[h, pl.ds(i1, 1), :].astype(BF16) * e2_ref[h]
            w = w + jnp.where(hit, gate, jnp.zeros((), BF16))
        w_ref[sub * nk:(sub + 1) * nk, :] = w.astype(F32)

    chains = list(range(0, te, PEER_CHAIN))
    hts = [_dot(u_ref[c0:c0 + PEER_CHAIN, :], x_ref[...], _NT) for c0 in chains]
    for c0, ht in zip(chains, hts):
        gelu = 0.5 * ht * (1.0 + lax.erf(ht * (2.0 ** -0.5)))
        a = (gelu * w_ref[c0:c0 + PEER_CHAIN, :]).astype(BF16)
        o_ref[...] += _dot(vt_ref[:, c0:c0 + PEER_CHAIN], a)


def peer_dense(xb, u, vt, tables, tn=512, te=512):
    n, d = xb.shape
    n_exp = u.shape[0]
    tn = min(tn, n)
    assert te % PEER_CHAIN == 0 and n_exp % te == 0
    once = pl.Buffered(1)
    tspec = pl.BlockSpec((PEER_HEADS, PEER_N_KEYS, tn), lambda i, j: (0, 0, i), pipeline_mode=once)
    return pl.pallas_call(
        functools.partial(_peer_dense_kernel, te=te, tn=tn),
        grid=(n // tn, n_exp // te),
        in_specs=[pl.BlockSpec((tn, d), lambda i, j: (i, 0), pipeline_mode=once),
                  pl.BlockSpec((te, d), lambda i, j: (j, 0)),
                  pl.BlockSpec((d, te), lambda i, j: (0, j)),
                  tspec, tspec, tspec, tspec],
        out_specs=pl.BlockSpec((d, tn), lambda i, j: (0, i)),
        out_shape=jax.ShapeDtypeStruct((d, n), F32),
        scratch_shapes=[pltpu.VMEM((te, tn), F32)],
        compiler_params=_params("parallel", "arbitrary"),
        name="peer_dense",
    )(xb, u, vt, *tables)


def odd_mixer_core(hb, w_in, cmp_pe, cmp_w, batch, seq, nsa_tq=128):
    q_w = NSA_GROUPS * NSA_REP * LANES
    kv_w = NSA_GROUPS * LANES
    nsa_main = q_w + 6 * kv_w
    n_gate = 3 * NSA_GROUPS * NSA_REP
    z = matmul(hb, w_in[:, :nsa_main].astype(BF16), BF16, scaled_cols=q_w, scale=QK_LOG2_SCALE)
    per_group = n_gate // NSA_GROUPS
    w_gate = w_in[:, nsa_main:].reshape(-1, NSA_GROUPS, per_group)
    w_gate = jnp.pad(w_gate, ((0, 0), (0, 0), (0, LANES - per_group)))
    w_gate = w_gate.reshape(-1, NSA_GROUPS * LANES).astype(BF16)
    gates = matmul(hb, w_gate, F32, act="sigmoid")
    xc = z[:, q_w:q_w + 2 * kv_w].reshape(batch, seq // CMP_STRIDE, CMP_STRIDE, 2 * NSA_GROUPS, LANES)
    xc = xc.transpose(0, 3, 1, 2, 4).reshape(
        batch, 2 * NSA_GROUPS, seq // CMP_STRIDE, CMP_STRIDE * LANES)
    half = CMP_STRIDE * LANES
    cw = cmp_w.astype(BF16)
    wcat = jnp.concatenate([cw[:, :half], cw[:, half:]], axis=2)
    pe = cmp_pe.reshape(2, 2, 1, half).astype(BF16)
    pecat = jnp.concatenate([pe[:, 0], pe[:, 1], jnp.zeros((2, 6, half), BF16)], axis=1)
    kvc = nsa_compress(xc, wcat, pecat)
    return nsa_attention(z, kvc, gates, batch, seq, tq=nsa_tq)


def even_mixer_core(hb, w_in, lam_vec, diff_g, gamma, hgrn_g, batch, seq, lam_init, layer):
    diff_w = DIFF_HEADS * 2 * LANES
    attn_cols = 3 * diff_w
    w_in = w_in.astype(BF16)
    z_attn = matmul(hb, w_in[:, :attn_cols], BF16, scaled_cols=diff_w, scale=QK_LOG2_SCALE)
    z_rec = matmul(hb, w_in[:, attn_cols:], F32)
    o_a = diff_attention(z_attn, lam_vec, diff_g, batch, seq, lam_init)
    o_b = hgrn2(z_rec, gamma, hgrn_g, batch, seq, layer)
    return jnp.concatenate([o_a, o_b], axis=1)


def peer_core(hb, wq, keys, u, v):
    pq = matmul(hb, wq.astype(BF16), F32)
    tables = peer_route(pq, keys)
    return peer_dense(hb, u.astype(BF16), v.T.astype(BF16), tables)


def kernel(x, mem, w_in_even, w_out_even, diff_lambda, diff_norm_g, hgrn_gamma, hgrn_norm_g,
           w_in_odd, w_out_odd, cmp_pe, cmp_w, xattn_wq, xattn_wkv, xattn_wo, ln_g, ln_b,
           peer_wq, peer_keys, peer_u, peer_v):
    batch, seq, d = x.shape
    depth = ln_g.shape[0]
    n = batch * seq
    assert d == 32 * LANES, "head dims are d_model/32 and must equal the lane count"
    alpha = (2.0 * depth) ** 0.25

    h = x.reshape(n, d)
    hb = h.astype(BF16)
    memb = mem.reshape(-1, d).astype(BF16)

    for l in range(depth):
        j = l // 2
        if l % 2 == 0:
            lam_init = 0.8 - 0.6 * math.exp(-0.3 * l)
            mix = even_mixer_core(hb, w_in_even[j], diff_lambda[j], diff_norm_g[j], hgrn_gamma,
                                  hgrn_norm_g[j], batch, seq, lam_init, j)
            y = matmul(mix, w_out_even[j].astype(BF16), F32)
        else:
            o = odd_mixer_core(hb, w_in_odd[j], cmp_pe[j], cmp_w[j], batch, seq)
            y = matmul(o, w_out_odd[j].astype(BF16), F32)
        h, hb = ln_residual(h, y, ln_g[l, 0], ln_b[l, 0], alpha)

        kv = matmul(memb, xattn_wkv[l].astype(BF16), BF16)
        h, hb = xattn_block(hb, h, xattn_wq[l].astype(BF16), kv, xattn_wo[l].astype(BF16),
                            ln_g[l, 1], ln_b[l, 1], alpha, batch, seq)

        yt = peer_core(hb, peer_wq[l], peer_keys[l], peer_u[l], peer_v[l])
        h, hb = ln_residual(h, yt, ln_g[l, 2], ln_b[l, 2], alpha, y_transposed=True)

    return h.reshape(batch, seq, d)
```

```python
import functools
import math

import jax
import jax.numpy as jnp
from jax import lax
from jax.experimental import pallas as pl
from jax.experimental.pallas import tpu as pltpu

F32 = jnp.float32
BF16 = jnp.bfloat16
I32 = jnp.int32

LANES = 128
VMEM_LIMIT_BYTES = 56 * 1024 * 1024

DIFF_HEADS = 8
HGRN_HEADS = 16
HGRN_CHUNK = 16
LB_FLOOR = 1e-30
NSA_GROUPS = 4
NSA_REP = 8
CMP_LEN = 32
CMP_STRIDE = 16
SLC_LEN = 64
SLC_TOPN = 16
WINDOW = 512
FORCE_SCORE = 1e9
XATTN_HEADS = 4
PEER_HEADS = 8
PEER_N_KEYS = 128
PEER_TOPK = 16
FLASH_LOOKAHEAD = 3
PEER_CHAIN = 256
QK_LOG2_SCALE = LANES ** -0.5 * math.log2(math.e)
LN_EPS = 1e-5
RMS_EPS = 1e-6
NEG_INF = -1e30

_NT = (((1,), (1,)), ((), ()))
_TN = (((0,), (0,)), ((), ()))


def _params(*sem):
    return pltpu.CompilerParams(dimension_semantics=sem, vmem_limit_bytes=VMEM_LIMIT_BYTES)


def _dot(a, b, dims=None, precision=None):
    if dims is None:
        return jnp.dot(a, b, preferred_element_type=F32, precision=precision)
    return lax.dot_general(a, b, dims, preferred_element_type=F32, precision=precision)


def _matmul_kernel(x_ref, w_ref, o_ref, *, act, scaled_blocks, scale):
    acc = _dot(x_ref[...], w_ref[...])
    if scaled_blocks:
        acc = acc * jnp.where(pl.program_id(1) < scaled_blocks, scale, 1.0)
    if act == "sigmoid":
        acc = jax.nn.sigmoid(acc)
    o_ref[...] = acc.astype(o_ref.dtype)


def matmul(x, w, out_dtype, act=None, scaled_cols=0, scale=1.0, tm=1024, tn=1024):
    m, k = x.shape
    n = w.shape[1]
    tm = min(tm, m)
    tn = min(tn, n)
    assert m % tm == 0 and n % tn == 0 and scaled_cols % tn == 0, (m, n, tm, tn, scaled_cols)
    return pl.pallas_call(
        functools.partial(_matmul_kernel, act=act, scaled_blocks=scaled_cols // tn, scale=scale),
        grid=(m // tm, n // tn),
        in_specs=[pl.BlockSpec((tm, k), lambda i, j: (i, 0)),
                  pl.BlockSpec((k, tn), lambda i, j: (0, j))],
        out_specs=pl.BlockSpec((tm, tn), lambda i, j: (i, j)),
        out_shape=jax.ShapeDtypeStruct((m, n), out_dtype),
        compiler_params=_params("parallel", "parallel"),
        name="matmul",
    )(x, w)


def _layer_norm_rows(z, g, b):
    mu = jnp.mean(z, axis=-1, keepdims=True)
    zc = z - mu
    var = jnp.mean(zc * zc, axis=-1, keepdims=True)
    return zc * lax.rsqrt(var + LN_EPS) * g + b


def _ln_res_kernel(h_ref, y_ref, g_ref, b_ref, o_ref, ob_ref, *, alpha, y_transposed):
    y = y_ref[...]
    if y_transposed:
        y = y.T
    out = _layer_norm_rows(alpha * h_ref[...] + y, g_ref[...], b_ref[...])
    o_ref[...] = out
    ob_ref[...] = out.astype(BF16)


def ln_residual(h, y, g, b, alpha, y_transposed=False, tm=256):
    n, d = h.shape
    tm = min(tm, n)
    y_spec = (pl.BlockSpec((d, tm), lambda i: (0, i)) if y_transposed
              else pl.BlockSpec((tm, d), lambda i: (i, 0)))
    return pl.pallas_call(
        functools.partial(_ln_res_kernel, alpha=alpha, y_transposed=y_transposed),
        grid=(n // tm,),
        in_specs=[pl.BlockSpec((tm, d), lambda i: (i, 0)), y_spec,
                  pl.BlockSpec((1, d), lambda i: (0, 0)),
                  pl.BlockSpec((1, d), lambda i: (0, 0))],
        out_specs=[pl.BlockSpec((tm, d), lambda i: (i, 0)),
                   pl.BlockSpec((tm, d), lambda i: (i, 0))],
        out_shape=[jax.ShapeDtypeStruct((n, d), F32), jax.ShapeDtypeStruct((n, d), BF16)],
        compiler_params=_params("parallel"),
        name="ln_residual",
    )(h, y, g.reshape(1, d), b.reshape(1, d))


def _lane_tile(x, width):
    reps = width // LANES
    return x if reps == 1 else jnp.concatenate([x] * reps, axis=1)


def _flash_heads(score_fns, v, m_ref, l_ref, acc_ref, psum_ref=None, row_ok=None):
    n = len(score_fns)
    dv = acc_ref.shape[-1]
    ahead = min(FLASH_LOOKAHEAD, n)
    scores = [score_fns[r]() for r in range(ahead)]
    pending = []

    def retire():
        pr, pa, ppv = pending.pop(0)
        acc_ref[pr] = _lane_tile(pa, dv) * acc_ref[pr] + ppv

    for r in range(n):
        s = scores.pop(0)
        if r + ahead < n:
            scores.append(score_fns[r + ahead]())
        m_prev = m_ref[r]
        m_next = jnp.maximum(m_prev, jnp.max(s, axis=-1, keepdims=True))
        p = jnp.exp2(s - _lane_tile(m_next, s.shape[1]))
        alpha = jnp.exp2(m_prev - m_next)
        l_new = alpha * l_ref[r] + jnp.sum(p, axis=-1, keepdims=True)
        l_ref[r] = l_new
        m_ref[r] = m_next
        if psum_ref is not None:
            psum_ref[...] += p * _lane_tile(jnp.where(row_ok, 1.0 / l_new, 0.0), s.shape[1])
        if len(pending) == ahead:
            retire()
        pending.append((r, alpha, _dot(p.astype(BF16), v)))
    while pending:
        retire()


def _diff_attn_kernel(lam_ref, q_ref, k_ref, v_ref, g_ref, o_ref, m_ref, l_ref, acc_ref,
                      *, tq, tk, lam_init):
    qi = pl.program_id(2)
    m_ref[...] = jnp.full(m_ref.shape, NEG_INF, F32)
    l_ref[...] = jnp.zeros(l_ref.shape, F32)
    acc_ref[...] = jnp.zeros(acc_ref.shape, F32)

    def chunk(c, bias):
        k0 = pl.multiple_of(c * tk, tk)
        v = v_ref[pl.ds(k0, tk), :]

        def scores(half):
            sl = slice(half * LANES, (half + 1) * LANES)
            s = _dot(q_ref[:, sl], k_ref[pl.ds(k0, tk), sl], _NT)
            return s if bias is None else s + bias

        _flash_heads([functools.partial(scores, half) for half in range(2)],
                     v, m_ref, l_ref, acc_ref)

    def body(c, carry):
        chunk(c, None)
        return carry

    lax.fori_loop(0, qi, body, 0)
    causal = (lax.broadcasted_iota(I32, (1, tk), 1) <= lax.broadcasted_iota(I32, (tq, 1), 0))
    chunk(qi, jnp.where(causal, 0.0, NEG_INF))

    lv = lam_ref[...]
    lam = (jnp.exp(jnp.sum(lv[0:1] * lv[1:2], axis=-1, keepdims=True))
           - jnp.exp(jnp.sum(lv[2:3] * lv[3:4], axis=-1, keepdims=True)) + lam_init)
    dv = acc_ref.shape[-1]
    o = (acc_ref[0] / _lane_tile(l_ref[0], dv) - lam * (acc_ref[1] / _lane_tile(l_ref[1], dv)))
    o = o * lax.rsqrt(jnp.mean(o * o, axis=-1, keepdims=True) + RMS_EPS)
    o_ref[...] = (o * g_ref[...] * (1.0 - lam_init)).astype(o_ref.dtype)


def diff_attention(z, lam_vec, norm_g, batch, seq, lam_init, tq=512):
    dv = 2 * LANES
    tq = tk = min(tq, seq)
    nq = seq // tq
    return pl.pallas_call(
        functools.partial(_diff_attn_kernel, tq=tq, tk=tk, lam_init=lam_init),
        grid=(batch, DIFF_HEADS, nq),
        in_specs=[pl.BlockSpec((4, LANES), lambda b, h, i: (0, 0)),
                  pl.BlockSpec((tq, dv), lambda b, h, i: (b * nq + i, h)),
                  pl.BlockSpec((seq, dv), lambda b, h, i: (b, DIFF_HEADS + h)),
                  pl.BlockSpec((seq, dv), lambda b, h, i: (b, 2 * DIFF_HEADS + h)),
                  pl.BlockSpec((1, dv), lambda b, h, i: (0, 0))],
        out_specs=pl.BlockSpec((tq, dv), lambda b, h, i: (b * nq + i, h)),
        out_shape=jax.ShapeDtypeStruct((batch * seq, DIFF_HEADS * dv), BF16),
        scratch_shapes=[pltpu.VMEM((2, tq, LANES), F32), pltpu.VMEM((2, tq, LANES), F32),
                        pltpu.VMEM((2, tq, dv), F32)],
        compiler_params=_params("parallel", "parallel", "arbitrary"),
        name="diff_attention",
    )(lam_vec, z, z, z, norm_g.reshape(1, dv))


def _hgrn_kernel(gam_ref, q_ref, f_ref, i_ref, gate_ref, ng_ref, o_ref,
                 st_ref, b_ref, qh_ref, kin_ref, *, tt, layer, heads):
    c_len = HGRN_CHUNK
    width = heads * LANES

    @pl.when(pl.program_id(2) == 0)
    def _():
        st_ref[...] = jnp.zeros(st_ref.shape, F32)

    gam = gam_ref[...]
    e = jnp.exp(gam - jnp.max(gam, axis=0, keepdims=True))
    p = e / jnp.sum(e, axis=0, keepdims=True)
    lb = jnp.zeros((1, width), F32)
    for r in range(1, layer + 1):
        lb = lb + p[r:r + 1]
    log_lb = jnp.log(jnp.maximum(lb, LB_FLOOR))
    z = f_ref[...]
    log_sig = jnp.minimum(z, 0.0) - jnp.log1p(jnp.exp(-jnp.abs(z)))
    log_gate = jnp.log1p(-lb) + log_sig
    log_f = (jnp.maximum(log_lb, log_gate)
             + jnp.log1p(jnp.exp(-jnp.abs(log_lb - log_gate))))
    kin_ref[...] = (1.0 - lb) * jax.nn.sigmoid(-z)
    qh_ref[...] = jax.nn.silu(q_ref[...])
    row = lax.broadcasted_iota(I32, (tt, 1), 0) % c_len
    b = log_f
    step = 1
    while step < c_len:
        b = b + jnp.where(row >= step, pltpu.roll(b, step, axis=0), 0.0)
        step *= 2
    b_ref[...] = b
    trow = lax.broadcasted_iota(I32, (c_len, 1), 0)
    ng = ng_ref[...]

    def body(c, carry):
        rows = pl.ds(pl.multiple_of(c * c_len, c_len), c_len)
        for hd in range(heads):
            cols = slice(hd * LANES, (hd + 1) * LANES)
            bc = b_ref[rows, cols]
            qc = qh_ref[rows, cols]
            kc = kin_ref[rows, cols]
            ic = i_ref[rows, cols]
            st = st_ref[hd]
            o = _dot((qc * jnp.exp(bc)).astype(BF16), st.astype(BF16), _NT)
            for s in range(c_len):
                es = jnp.where(trow >= s, jnp.exp(bc - bc[s:s + 1]), 0.0)
                col = jnp.sum(qc * es * kc[s:s + 1], axis=-1, keepdims=True)
                o = o + col * ic[s:s + 1]
            b_last = bc[c_len - 1:c_len]
            kd = kc * jnp.exp(b_last - bc)
            st_ref[hd] = st * jnp.exp(b_last) + _dot(ic.astype(BF16), kd.astype(BF16), _TN)
            y = o * lax.rsqrt(jnp.mean(o * o, axis=-1, keepdims=True) + RMS_EPS) * ng
            gate = jax.nn.silu(gate_ref[rows, cols])
            o_ref[rows, cols] = (y * gate).astype(o_ref.dtype)
        return carry

    lax.fori_loop(0, tt // c_len, body, 0)


def hgrn2(z, gamma, norm_g, batch, seq, layer, tt=512, heads=4):
    tt = min(tt, seq)
    nt = seq // tt
    n_even = gamma.shape[0]
    ng = HGRN_HEADS // heads
    width = heads * LANES

    def col(offset):
        return pl.BlockSpec((tt, width), lambda b, h, t: (b * nt + t, offset * ng + h))

    tile = pltpu.VMEM((tt, width), F32)
    return pl.pallas_call(
        functools.partial(_hgrn_kernel, tt=tt, layer=layer, heads=heads),
        grid=(batch, ng, nt),
        in_specs=[pl.BlockSpec((n_even, width), lambda b, h, t: (0, h)),
                  col(0), col(1), col(2), col(3),
                  pl.BlockSpec((1, LANES), lambda b, h, t: (0, 0))],
        out_specs=pl.BlockSpec((tt, width), lambda b, h, t: (b * nt + t, h)),
        out_shape=jax.ShapeDtypeStruct((batch * seq, HGRN_HEADS * LANES), BF16),
        scratch_shapes=[pltpu.VMEM((heads, LANES, LANES), F32), tile, tile, tile],
        compiler_params=_params("parallel", "parallel", "arbitrary"),
        name="hgrn2",
    )(gamma, z, z, z, z, norm_g.reshape(1, LANES))


def _topk_rows(vals, k, ids=None):
    r, n = vals.shape
    iota0 = lax.broadcasted_iota(I32, (r, n), 0) if ids is None else ids
    iota_k = lax.broadcasted_iota(I32, (k, n), 0)
    rank = jnp.full((r, n), k, I32)
    tops = []
    stacked = jnp.zeros((k, n), F32)
    for a in range(k):
        m = jnp.max(vals, axis=0, keepdims=True)
        idx = jnp.min(jnp.where(vals == m, iota0, jnp.iinfo(jnp.int32).max), axis=0, keepdims=True)
        hit = iota0 == idx
        rank = jnp.where(hit, a, rank)
        tops.append(m)
        stacked = jnp.where(iota_k == a, m, stacked)
        vals = jnp.where(hit, -jnp.inf, vals)
    return tops, stacked, rank


def _compress_kernel(x_ref, w_ref, pe_ref, o_ref):
    pq = _dot(x_ref[...], w_ref[...])
    nrow = pq.shape[0]
    pe_c = _dot(pe_ref[...], w_ref[...])
    const = pe_c[0:1, :LANES] + pe_c[1:2, LANES:]
    nxt = pltpu.roll(pq[:, LANES:], nrow - 1, axis=0)
    o_ref[...] = (pq[:, :LANES] + nxt + const).astype(o_ref.dtype)


def nsa_compress(xc, wcat, pecat):
    bsz, nj, nrow, kk = xc.shape
    return pl.pallas_call(
        _compress_kernel,
        grid=(bsz, nj),
        in_specs=[pl.BlockSpec((None, None, nrow, kk), lambda b, j: (b, j, 0, 0)),
                  pl.BlockSpec((None, kk, 2 * LANES), lambda b, j: (j // NSA_GROUPS, 0, 0)),
                  pl.BlockSpec((None, 8, kk), lambda b, j: (j // NSA_GROUPS, 0, 0))],
        out_specs=pl.BlockSpec((None, None, nrow, LANES), lambda b, j: (b, j, 0, 0)),
        out_shape=jax.ShapeDtypeStruct((bsz, nj, nrow, LANES), BF16),
        compiler_params=_params("parallel", "parallel"),
        name="nsa_compress",
    )(xc, wcat, pecat)


def _nsa_kernel(q_ref, kc_ref, vc_ref, ks_ref, vs_ref, kw_ref, vw_ref, gt_ref, o_ref,
                m_ref, l_ref, acc_ref, mw_ref, lw_ref, accw_ref, mc_ref, lc_ref, accc_ref,
                psum_ref, sel_ref, *, tq, tk, ncp, nsp, n_top):
    rep = NSA_REP
    q0 = pl.program_id(2) * tq
    qpos = q0 + lax.broadcasted_iota(I32, (tq, 1), 0)

    def q_head(r):
        return q_ref[:, r * LANES:(r + 1) * LANES]

    for ref in (m_ref, mw_ref, mc_ref):
        ref[...] = jnp.full(ref.shape, NEG_INF, F32)
    for ref in (l_ref, acc_ref, lw_ref, accw_ref, lc_ref, accc_ref):
        ref[...] = jnp.zeros(ref.shape, F32)

    def once(fn):
        lax.fori_loop(0, jnp.minimum(pl.program_id(2), 0) + 1, lambda _, carry: (fn(), carry)[1], 0)

    has_key = qpos >= CMP_LEN - 1

    def compressed():
        cend = lax.broadcasted_iota(I32, (1, ncp), 1) * CMP_STRIDE + (CMP_LEN - 1)
        bias_c = jnp.where(cend <= qpos, 0.0, NEG_INF)
        psum_ref[...] = jnp.zeros(psum_ref.shape, F32)
        kc = kc_ref[...]
        _flash_heads([functools.partial(lambda r: _dot(q_head(r), kc, _NT) + bias_c, r)
                      for r in range(rep)], vc_ref[...], mc_ref, lc_ref, accc_ref, psum_ref, has_key)

    once(compressed)

    n_i = lax.broadcasted_iota(I32, (ncp, nsp), 0)
    j4 = lax.broadcasted_iota(I32, (ncp, nsp), 1) * (SLC_LEN // CMP_STRIDE)
    overlap = ((n_i >= j4 - (CMP_LEN - 1) // CMP_STRIDE)
               & (n_i < j4 + SLC_LEN // CMP_STRIDE)).astype(F32)
    imp_t = _dot(psum_ref[...], overlap, precision=lax.Precision.HIGHEST).T
    jt = lax.broadcasted_iota(I32, (nsp, 1), 0)
    cur = (q0 + lax.broadcasted_iota(I32, (1, tq), 1)) // SLC_LEN
    forced = (jt == 0) | (jt == cur) | (jt == cur - 1)
    imp_t = jnp.where(forced, FORCE_SCORE, jnp.where(jt <= cur, imp_t, -1.0))
    _, _, rank = _topk_rows(imp_t, n_top)
    sel_ref[...] = (rank < n_top).astype(F32).T.astype(BF16)

    jj = lax.broadcasted_iota(I32, (nsp, 1), 0)

    def body(c, carry):
        k0 = pl.multiple_of(c * tk, tk)
        k = ks_ref[pl.ds(k0, tk), :]
        v = vs_ref[pl.ds(k0, tk), :]
        kpos = k0 + lax.broadcasted_iota(I32, (1, tk), 1)
        expand = (jj == kpos // SLC_LEN).astype(BF16)
        allowed = (_dot(sel_ref[...], expand) > 0.5) & (kpos <= qpos)
        bias = jnp.where(allowed, 0.0, NEG_INF)
        _flash_heads([functools.partial(lambda r: _dot(q_head(r), k, _NT) + bias, r)
                      for r in range(rep)], v, m_ref, l_ref, acc_ref)
        return carry

    lax.fori_loop(0, (q0 + tq + tk - 1) // tk, body, 0)

    span = WINDOW + tq
    start = pl.multiple_of(jnp.maximum(q0 - WINDOW, 0), tq)

    def window():
        kpos = start + lax.broadcasted_iota(I32, (1, span), 1)
        bias_w = jnp.where((kpos <= qpos) & (kpos > qpos - WINDOW), 0.0, NEG_INF)
        k_w = kw_ref[pl.ds(start, span), :]
        _flash_heads([functools.partial(lambda r: _dot(q_head(r), k_w, _NT) + bias_w, r)
                      for r in range(rep)], vw_ref[pl.ds(start, span), :], mw_ref, lw_ref, accw_ref)

    once(window)

    gt = gt_ref[...]
    for r in range(rep):
        o = (gt[:, 3 * r:3 * r + 1] * jnp.where(has_key, accc_ref[r] / lc_ref[r], 0.0)
             + gt[:, 3 * r + 1:3 * r + 2] * (acc_ref[r] / l_ref[r])
             + gt[:, 3 * r + 2:3 * r + 3] * (accw_ref[r] / lw_ref[r]))
        o_ref[:, r * LANES:(r + 1) * LANES] = o.astype(o_ref.dtype)


def nsa_attention(z, kvc, gates, batch, seq, tq=128, tk=512):
    tk = min(tk, seq)
    assert seq % tq == 0 and seq >= WINDOW + tq and tq % SLC_LEN == 0 and seq % tk == 0
    nq = seq // tq
    ncp = seq // CMP_STRIDE
    n_slc = seq // SLC_LEN
    nsp = -(-n_slc // LANES) * LANES
    n_top = min(SLC_TOPN, n_slc)
    qw = NSA_REP * LANES
    g = NSA_GROUPS
    q_cols = g * NSA_REP

    def kv_col(which):
        return pl.BlockSpec((seq, LANES), lambda b, gi, i: (b, q_cols + which * g + gi))

    head_state = pltpu.VMEM((NSA_REP, tq, LANES), F32)
    return pl.pallas_call(
        functools.partial(_nsa_kernel, tq=tq, tk=tk, ncp=ncp, nsp=nsp, n_top=n_top),
        grid=(batch, g, nq),
        in_specs=[pl.BlockSpec((tq, qw), lambda b, gi, i: (b * nq + i, gi)),
                  pl.BlockSpec((None, None, ncp, LANES), lambda b, gi, i: (b, gi, 0, 0)),
                  pl.BlockSpec((None, None, ncp, LANES), lambda b, gi, i: (b, g + gi, 0, 0)),
                  kv_col(2), kv_col(3), kv_col(4), kv_col(5),
                  pl.BlockSpec((tq, LANES), lambda b, gi, i: (b * nq + i, gi))],
        out_specs=pl.BlockSpec((tq, qw), lambda b, gi, i: (b * nq + i, gi)),
        out_shape=jax.ShapeDtypeStruct((batch * seq, g * qw), BF16),
        scratch_shapes=[head_state] * 9 + [pltpu.VMEM((tq, ncp), F32), pltpu.VMEM((tq, nsp), BF16)],
        compiler_params=_params("parallel", "parallel", "arbitrary"),
        name="nsa_attention",
    )(z, kvc, kvc, z, z, z, z, gates)


def _xattn_kernel(hb_ref, h_ref, wq_ref, kv_ref, wo_ref, g_ref, b_ref, o_ref, ob_ref, *, alpha):
    width = XATTN_HEADS * LANES
    scale = LANES ** -0.5
    q = _dot(hb_ref[...], wq_ref[...]).astype(BF16)
    kv = kv_ref[...]
    outs = []
    for hd in range(XATTN_HEADS):
        sl = slice(hd * LANES, (hd + 1) * LANES)
        s = _dot(q[:, sl], kv[:, sl], _NT) * scale
        e = jnp.exp(s - jnp.max(s, axis=-1, keepdims=True))
        p = e / jnp.sum(e, axis=-1, keepdims=True)
        outs.append(_dot(p.astype(BF16), kv[:, width + hd * LANES:width + (hd + 1) * LANES]))
    o = jnp.concatenate(outs, axis=1).astype(BF16)
    y = _dot(o, wo_ref[...])
    out = _layer_norm_rows(alpha * h_ref[...] + y, g_ref[...], b_ref[...])
    o_ref[...] = out
    ob_ref[...] = out.astype(BF16)


def xattn_block(hb, h, wq, kv, wo, g, b, alpha, batch, seq, tm=256):
    n, d = h.shape
    tm = min(tm, seq)
    nt = seq // tm
    mem_len = kv.shape[0] // batch
    width = XATTN_HEADS * LANES
    return pl.pallas_call(
        functools.partial(_xattn_kernel, alpha=alpha),
        grid=(batch, nt),
        in_specs=[pl.BlockSpec((tm, d), lambda bi, i: (bi * nt + i, 0)),
                  pl.BlockSpec((tm, d), lambda bi, i: (bi * nt + i, 0)),
                  pl.BlockSpec((d, width), lambda bi, i: (0, 0)),
                  pl.BlockSpec((mem_len, 2 * width), lambda bi, i: (bi, 0)),
                  pl.BlockSpec((width, d), lambda bi, i: (0, 0)),
                  pl.BlockSpec((1, d), lambda bi, i: (0, 0)),
                  pl.BlockSpec((1, d), lambda bi, i: (0, 0))],
        out_specs=[pl.BlockSpec((tm, d), lambda bi, i: (bi * nt + i, 0)),
                   pl.BlockSpec((tm, d), lambda bi, i: (bi * nt + i, 0))],
        out_shape=[jax.ShapeDtypeStruct((n, d), F32), jax.ShapeDtypeStruct((n, d), BF16)],
        compiler_params=_params("parallel", "parallel"),
        name="xattn_block",
    )(hb, h, wq, kv, wo, g.reshape(1, d), b.reshape(1, d))


def _peer_route_kernel(q_ref, keys_ref, mb_ref, e1_ref, b2_ref, e2_ref, *, tn):
    k = PEER_TOPK
    nk = PEER_N_KEYS
    hi = lax.Precision.HIGHEST
    for h in range(PEER_HEADS):
        s1 = _dot(keys_ref[h, 0], q_ref[:, (2 * h) * LANES:(2 * h + 1) * LANES], _NT, hi)
        s2 = _dot(keys_ref[h, 1], q_ref[:, (2 * h + 1) * LANES:(2 * h + 2) * LANES], _NT, hi)
        v1, v1_all, rank1 = _topk_rows(s1, k)
        v2, v2_all, rank2 = _topk_rows(s2, k)
        half = k // 2
        iota_a = lax.broadcasted_iota(I32, (k, tn), 0)
        is_hi = iota_a >= half
        cand = [v1_all + v2[b] for b in range(half)] + [jnp.where(is_hi, v1[0] + v2_all, -jnp.inf)]
        ids = [iota_a * k + b for b in range(half)] + [jnp.where(is_hi, iota_a, iota_a + k * k)]
        tops, _, crank = _topk_rows(jnp.concatenate(cand, axis=0), k, jnp.concatenate(ids, axis=0))
        zsum = tops[0] * 0.0
        for a in range(k):
            zsum = zsum + jnp.exp(tops[a] - tops[0])
        cnt = jnp.zeros((k, tn), F32)
        for b in range(half):
            cnt = cnt + jnp.where(crank[b * k:(b + 1) * k] < k, 1.0, 0.0)
        won_hi = (crank[half * k:(half + 1) * k] < k) & is_hi
        extra = jnp.sum(jnp.where(won_hi, 1.0, 0.0), axis=0, keepdims=True)
        cnt = (cnt + jnp.where(iota_a == 0, extra, 0.0)).astype(I32)
        mb = jnp.zeros((nk, tn), I32)
        for a in range(k):
            mb = jnp.where(rank1 == a, cnt[a:a + 1], mb)
        mb_ref[h] = mb
        b2_ref[h] = rank2.astype(b2_ref.dtype)
        e1_ref[h] = jnp.where(rank1 < k, jnp.exp(s1 - v1[0]), 0.0) / zsum
        e2_ref[h] = jnp.where(rank2 < k, jnp.exp(s2 - v2[0]), 0.0).astype(e2_ref.dtype)


def peer_route(q, keys, tn=256):
    n = q.shape[0]
    tn = min(tn, n)
    tab = lambda dt: jax.ShapeDtypeStruct((PEER_HEADS, PEER_N_KEYS, n), dt)
    tspec = pl.BlockSpec((PEER_HEADS, PEER_N_KEYS, tn), lambda i: (0, 0, i))
    return pl.pallas_call(
        functools.partial(_peer_route_kernel, tn=tn),
        grid=(n // tn,),
        in_specs=[pl.BlockSpec((tn, q.shape[1]), lambda i: (i, 0)),
                  pl.BlockSpec(keys.shape, lambda i: (0, 0, 0, 0))],
        out_specs=[tspec, tspec, tspec, tspec],
        out_shape=[tab(I32), tab(F32), tab(jnp.int16), tab(BF16)],
        compiler_params=_params("parallel"),
        name="peer_route",
    )(q, keys)


def _peer_dense_kernel(x_ref, u_ref, vt_ref, mb_ref, e1_ref, b2_ref, e2_ref, o_ref, w_ref,
                       *, te, tn):
    j = pl.program_id(1)

    @pl.when(j == 0)
    def _():
        o_ref[...] = jnp.zeros(o_ref.shape, F32)

    nk = PEER_N_KEYS
    for sub in range(te // nk):
        i1 = j * (te // nk) + sub
        w = jnp.zeros((nk, tn), BF16)
        for h in range(PEER_HEADS):
            hit = b2_ref[h] < mb_ref[h, pl.ds(i1, 1), :].astype(jnp.int16)
            gate = e1_ref[h, pl.ds(i1, 1), :].astype(BF16) * e2_ref[h]
            w = w + jnp.where(hit, gate, jnp.zeros((), BF16))
        w_ref[sub * nk:(sub + 1) * nk, :] = w.astype(F32)

    chains = list(range(0, te, PEER_CHAIN))
    hts = [_dot(u_ref[c0:c0 + PEER_CHAIN, :], x_ref[...], _NT) for c0 in chains]
    for c0, ht in zip(chains, hts):
        gelu = 0.5 * ht * (1.0 + lax.erf(ht * (2.0 ** -0.5)))
        a = (gelu * w_ref[c0:c0 + PEER_CHAIN, :]).astype(BF16)
        o_ref[...] += _dot(vt_ref[:, c0:c0 + PEER_CHAIN], a)


def peer_dense(xb, u, vt, tables, tn=512, te=512):
    n, d = xb.shape
    n_exp = u.shape[0]
    tn = min(tn, n)
    assert te % PEER_CHAIN == 0 and n_exp % te == 0
    once = pl.Buffered(1)
    tspec = pl.BlockSpec((PEER_HEADS, PEER_N_KEYS, tn), lambda i, j: (0, 0, i), pipeline_mode=once)
    return pl.pallas_call(
        functools.partial(_peer_dense_kernel, te=te, tn=tn),
        grid=(n // tn, n_exp // te),
        in_specs=[pl.BlockSpec((tn, d), lambda i, j: (i, 0), pipeline_mode=once),
                  pl.BlockSpec((te, d), lambda i, j: (j, 0)),
                  pl.BlockSpec((d, te), lambda i, j: (0, j)),
                  tspec, tspec, tspec, tspec],
        out_specs=pl.BlockSpec((d, tn), lambda i, j: (0, i)),
        out_shape=jax.ShapeDtypeStruct((d, n), F32),
        scratch_shapes=[pltpu.VMEM((te, tn), F32)],
        compiler_params=_params("parallel", "arbitrary"),
        name="peer_dense",
    )(xb, u, vt, *tables)


def odd_mixer_core(hb, w_in, cmp_pe, cmp_w, batch, seq, nsa_tq=128):
    q_w = NSA_GROUPS * NSA_REP * LANES
    kv_w = NSA_GROUPS * LANES
    nsa_main = q_w + 6 * kv_w
    n_gate = 3 * NSA_GROUPS * NSA_REP
    z = matmul(hb, w_in[:, :nsa_main].astype(BF16), BF16, scaled_cols=q_w, scale=QK_LOG2_SCALE)
    per_group = n_gate // NSA_GROUPS
    w_gate = w_in[:, nsa_main:].reshape(-1, NSA_GROUPS, per_group)
    w_gate = jnp.pad(w_gate, ((0, 0), (0, 0), (0, LANES - per_group)))
    w_gate = w_gate.reshape(-1, NSA_GROUPS * LANES).astype(BF16)
    gates = matmul(hb, w_gate, F32, act="sigmoid")
    xc = z[:, q_w:q_w + 2 * kv_w].reshape(batch, seq // CMP_STRIDE, CMP_STRIDE, 2 * NSA_GROUPS, LANES)
    xc = xc.transpose(0, 3, 1, 2, 4).reshape(
        batch, 2 * NSA_GROUPS, seq // CMP_STRIDE, CMP_STRIDE * LANES)
    half = CMP_STRIDE * LANES
    cw = cmp_w.astype(BF16)
    wcat = jnp.concatenate([cw[:, :half], cw[:, half:]], axis=2)
    pe = cmp_pe.reshape(2, 2, 1, half).astype(BF16)
    pecat = jnp.concatenate([pe[:, 0], pe[:, 1], jnp.zeros((2, 6, half), BF16)], axis=1)
    kvc = nsa_compress(xc, wcat, pecat)
    return nsa_attention(z, kvc, gates, batch, seq, tq=nsa_tq)


def even_mixer_core(hb, w_in, lam_vec, diff_g, gamma, hgrn_g, batch, seq, lam_init, layer):
    diff_w = DIFF_HEADS * 2 * LANES
    attn_cols = 3 * diff_w
    w_in = w_in.astype(BF16)
    z_attn = matmul(hb, w_in[:, :attn_cols], BF16, scaled_cols=diff_w, scale=QK_LOG2_SCALE)
    z_rec = matmul(hb, w_in[:, attn_cols:], F32)
    o_a = diff_attention(z_attn, lam_vec, diff_g, batch, seq, lam_init)
    o_b = hgrn2(z_rec, gamma, hgrn_g, batch, seq, layer)
    return jnp.concatenate([o_a, o_b], axis=1)


def peer_core(hb, wq, keys, u, v):
    pq = matmul(hb, wq.astype(BF16), F32)
    tables = peer_route(pq, keys)
    return peer_dense(hb, u.astype(BF16), v.T.astype(BF16), tables)


def kernel(x, mem, w_in_even, w_out_even, diff_lambda, diff_norm_g, hgrn_gamma, hgrn_norm_g,
           w_in_odd, w_out_odd, cmp_pe, cmp_w, xattn_wq, xattn_wkv, xattn_wo, ln_g, ln_b,
           peer_wq, peer_keys, peer_u, peer_v):
    batch, seq, d = x.shape
    depth = ln_g.shape[0]
    n = batch * seq
    assert d == 32 * LANES, "head dims are d_model/32 and must equal the lane count"
    alpha = (2.0 * depth) ** 0.25

    h = x.reshape(n, d)
    hb = h.astype(BF16)
    memb = mem.reshape(-1, d).astype(BF16)

    for l in range(depth):
        j = l // 2
        if l % 2 == 0:
            lam_init = 0.8 - 0.6 * math.exp(-0.3 * l)
            mix = even_mixer_core(hb, w_in_even[j], diff_lambda[j], diff_norm_g[j], hgrn_gamma,
                                  hgrn_norm_g[j], batch, seq, lam_init, j)
            y = matmul(mix, w_out_even[j].astype(BF16), F32)
        else:
            o = odd_mixer_core(hb, w_in_odd[j], cmp_pe[j], cmp_w[j], batch, seq)
            y = matmul(o, w_out_odd[j].astype(BF16), F32)
        h, hb = ln_residual(h, y, ln_g[l, 0], ln_b[l, 0], alpha)

        kv = matmul(memb, xattn_wkv[l].astype(BF16), BF16)
        h, hb = xattn_block(hb, h, xattn_wq[l].astype(BF16), kv, xattn_wo[l].astype(BF16),
                            ln_g[l, 1], ln_b[l, 1], alpha, batch, seq)

        yt = peer_core(hb, peer_wq[l], peer_keys[l], peer_u[l], peer_v[l])
        h, hb = ln_residual(h, yt, ln_g[l, 2], ln_b[l, 2], alpha, y_transposed=True)

    return h.reshape(batch, seq, d)
```

```python
import functools
import math

import jax
import jax.numpy as jnp
from jax import lax
from jax.experimental import pallas as pl
from jax.experimental.pallas import tpu as pltpu

F32 = jnp.float32
BF16 = jnp.bfloat16
I32 = jnp.int32

LANES = 128
VMEM_LIMIT_BYTES = 56 * 1024 * 1024

DIFF_HEADS = 8
HGRN_HEADS = 16
HGRN_CHUNK = 16
LB_FLOOR = 1e-30
NSA_GROUPS = 4
NSA_REP = 8
CMP_LEN = 32
CMP_STRIDE = 16
SLC_LEN = 64
SLC_TOPN = 16
WINDOW = 512
FORCE_SCORE = 1e9
XATTN_HEADS = 4
PEER_HEADS = 8
PEER_N_KEYS = 128
PEER_TOPK = 16
FLASH_LOOKAHEAD = 3
PEER_CHAIN = 256
QK_LOG2_SCALE = LANES ** -0.5 * math.log2(math.e)
LN_EPS = 1e-5
RMS_EPS = 1e-6
NEG_INF = -1e30

_NT = (((1,), (1,)), ((), ()))
_TN = (((0,), (0,)), ((), ()))


def _params(*sem):
    return pltpu.CompilerParams(dimension_semantics=sem, vmem_limit_bytes=VMEM_LIMIT_BYTES)


def _dot(a, b, dims=None, precision=None):
    if dims is None:
        return jnp.dot(a, b, preferred_element_type=F32, precision=precision)
    return lax.dot_general(a, b, dims, preferred_element_type=F32, precision=precision)


def _matmul_kernel(x_ref, w_ref, o_ref, *, act, scaled_blocks, scale):
    acc = _dot(x_ref[...], w_ref[...])
    if scaled_blocks:
        acc = acc * jnp.where(pl.program_id(1) < scaled_blocks, scale, 1.0)
    if act == "sigmoid":
        acc = jax.nn.sigmoid(acc)
    o_ref[...] = acc.astype(o_ref.dtype)


def matmul(x, w, out_dtype, act=None, scaled_cols=0, scale=1.0, tm=1024, tn=1024):
    m, k = x.shape
    n = w.shape[1]
    tm = min(tm, m)
    tn = min(tn, n)
    assert m % tm == 0 and n % tn == 0 and scaled_cols % tn == 0, (m, n, tm, tn, scaled_cols)
    return pl.pallas_call(
        functools.partial(_matmul_kernel, act=act, scaled_blocks=scaled_cols // tn, scale=scale),
        grid=(m // tm, n // tn),
        in_specs=[pl.BlockSpec((tm, k), lambda i, j: (i, 0)),
                  pl.BlockSpec((k, tn), lambda i, j: (0, j))],
        out_specs=pl.BlockSpec((tm, tn), lambda i, j: (i, j)),
        out_shape=jax.ShapeDtypeStruct((m, n), out_dtype),
        compiler_params=_params("parallel", "parallel"),
        name="matmul",
    )(x, w)


def _layer_norm_rows(z, g, b):
    mu = jnp.mean(z, axis=-1, keepdims=True)
    zc = z - mu
    var = jnp.mean(zc * zc, axis=-1, keepdims=True)
    return zc * lax.rsqrt(var + LN_EPS) * g + b


def _ln_res_kernel(h_ref, y_ref, g_ref, b_ref, o_ref, ob_ref, *, alpha, y_transposed):
    y = y_ref[...]
    if y_transposed:
        y = y.T
    out = _layer_norm_rows(alpha * h_ref[...] + y, g_ref[...], b_ref[...])
    o_ref[...] = out
    ob_ref[...] = out.astype(BF16)


def ln_residual(h, y, g, b, alpha, y_transposed=False, tm=256):
    n, d = h.shape
    tm = min(tm, n)
    y_spec = (pl.BlockSpec((d, tm), lambda i: (0, i)) if y_transposed
              else pl.BlockSpec((tm, d), lambda i: (i, 0)))
    return pl.pallas_call(
        functools.partial(_ln_res_kernel, alpha=alpha, y_transposed=y_transposed),
        grid=(n // tm,),
        in_specs=[pl.BlockSpec((tm, d), lambda i: (i, 0)), y_spec,
                  pl.BlockSpec((1, d), lambda i: (0, 0)),
                  pl.BlockSpec((1, d), lambda i: (0, 0))],
        out_specs=[pl.BlockSpec((tm, d), lambda i: (i, 0)),
                   pl.BlockSpec((tm, d), lambda i: (i, 0))],
        out_shape=[jax.ShapeDtypeStruct((n, d), F32), jax.ShapeDtypeStruct((n, d), BF16)],
        compiler_params=_params("parallel"),
        name="ln_residual",
    )(h, y, g.reshape(1, d), b.reshape(1, d))


def _lane_tile(x, width):
    reps = width // LANES
    return x if reps == 1 else jnp.concatenate([x] * reps, axis=1)


def _flash_heads(score_fns, v, m_ref, l_ref, acc_ref, psum_ref=None, row_ok=None):
    n = len(score_fns)
    dv = acc_ref.shape[-1]
    ahead = min(FLASH_LOOKAHEAD, n)
    scores = [score_fns[r]() for r in range(ahead)]
    pending = []

    def retire():
        pr, pa, ppv = pending.pop(0)
        acc_ref[pr] = _lane_tile(pa, dv) * acc_ref[pr] + ppv

    for r in range(n):
        s = scores.pop(0)
        if r + ahead < n:
            scores.append(score_fns[r + ahead]())
        m_prev = m_ref[r]
        m_next = jnp.maximum(m_prev, jnp.max(s, axis=-1, keepdims=True))
        p = jnp.exp2(s - _lane_tile(m_next, s.shape[1]))
        alpha = jnp.exp2(m_prev - m_next)
        l_new = alpha * l_ref[r] + jnp.sum(p, axis=-1, keepdims=True)
        l_ref[r] = l_new
        m_ref[r] = m_next
        if psum_ref is not None:
            psum_ref[...] += p * _lane_tile(jnp.where(row_ok, 1.0 / l_new, 0.0), s.shape[1])
        if len(pending) == ahead:
            retire()
        pending.append((r, alpha, _dot(p.astype(BF16), v)))
    while pending:
        retire()


def _diff_attn_kernel(lam_ref, q_ref, k_ref, v_ref, g_ref, o_ref, m_ref, l_ref, acc_ref,
                      *, tq, tk, lam_init):
    qi = pl.program_id(2)
    m_ref[...] = jnp.full(m_ref.shape, NEG_INF, F32)
    l_ref[...] = jnp.zeros(l_ref.shape, F32)
    acc_ref[...] = jnp.zeros(acc_ref.shape, F32)

    def chunk(c, bias):
        k0 = pl.multiple_of(c * tk, tk)
        v = v_ref[pl.ds(k0, tk), :]

        def scores(half):
            sl = slice(half * LANES, (half + 1) * LANES)
            s = _dot(q_ref[:, sl], k_ref[pl.ds(k0, tk), sl], _NT)
            return s if bias is None else s + bias

        _flash_heads([functools.partial(scores, half) for half in range(2)],
                     v, m_ref, l_ref, acc_ref)

    def body(c, carry):
        chunk(c, None)
        return carry

    lax.fori_loop(0, qi, body, 0)
    causal = (lax.broadcasted_iota(I32, (1, tk), 1) <= lax.broadcasted_iota(I32, (tq, 1), 0))
    chunk(qi, jnp.where(causal, 0.0, NEG_INF))

    lv = lam_ref[...]
    lam = (jnp.exp(jnp.sum(lv[0:1] * lv[1:2], axis=-1, keepdims=True))
           - jnp.exp(jnp.sum(lv[2:3] * lv[3:4], axis=-1, keepdims=True)) + lam_init)
    dv = acc_ref.shape[-1]
    o = (acc_ref[0] / _lane_tile(l_ref[0], dv) - lam * (acc_ref[1] / _lane_tile(l_ref[1], dv)))
    o = o * lax.rsqrt(jnp.mean(o * o, axis=-1, keepdims=True) + RMS_EPS)
    o_ref[...] = (o * g_ref[...] * (1.0 - lam_init)).astype(o_ref.dtype)


def diff_attention(z, lam_vec, norm_g, batch, seq, lam_init, tq=512):
    dv = 2 * LANES
    tq = tk = min(tq, seq)
    nq = seq // tq
    return pl.pallas_call(
        functools.partial(_diff_attn_kernel, tq=tq, tk=tk, lam_init=lam_init),
        grid=(batch, DIFF_HEADS, nq),
        in_specs=[pl.BlockSpec((4, LANES), lambda b, h, i: (0, 0)),
                  pl.BlockSpec((tq, dv), lambda b, h, i: (b * nq + i, h)),
                  pl.BlockSpec((seq, dv), lambda b, h, i: (b, DIFF_HEADS + h)),
                  pl.BlockSpec((seq, dv), lambda b, h, i: (b, 2 * DIFF_HEADS + h)),
                  pl.BlockSpec((1, dv), lambda b, h, i: (0, 0))],
        out_specs=pl.BlockSpec((tq, dv), lambda b, h, i: (b * nq + i, h)),
        out_shape=jax.ShapeDtypeStruct((batch * seq, DIFF_HEADS * dv), BF16),
        scratch_shapes=[pltpu.VMEM((2, tq, LANES), F32), pltpu.VMEM((2, tq, LANES), F32),
                        pltpu.VMEM((2, tq, dv), F32)],
        compiler_params=_params("parallel", "parallel", "arbitrary"),
        name="diff_attention",
    )(lam_vec, z, z, z, norm_g.reshape(1, dv))


def _hgrn_kernel(gam_ref, q_ref, f_ref, i_ref, gate_ref, ng_ref, o_ref,
                 st_ref, *, tt, layer, heads):
    c_len = HGRN_CHUNK
    width = heads * LANES

    @pl.when(pl.program_id(2) == 0)
    def _():
        st_ref[...] = jnp.zeros(st_ref.shape, F32)

    gam = gam_ref[...]
    e = jnp.exp(gam - jnp.max(gam, axis=0, keepdims=True))
    p = e / jnp.sum(e, axis=0, keepdims=True)
    lb = jnp.zeros((1, width), F32)
    for r in range(1, layer + 1):
        lb = lb + p[r:r + 1]
    log_lb = jnp.log(jnp.maximum(lb, LB_FLOOR))
    log_1m = jnp.log1p(-lb)
    trow = lax.broadcasted_iota(I32, (c_len, 1), 0)
    ng = ng_ref[...]

    def body(c, carry):
        rows = pl.ds(pl.multiple_of(c * c_len, c_len), c_len)
        z = f_ref[rows, :]
        log_sig = jnp.minimum(z, 0.0) - jnp.log1p(jnp.exp(-jnp.abs(z)))
        log_gate = log_1m + log_sig
        b_all = (jnp.maximum(log_lb, log_gate)
                 + jnp.log1p(jnp.exp(-jnp.abs(log_lb - log_gate))))
        step = 1
        while step < c_len:
            b_all = b_all + jnp.where(trow >= step, pltpu.roll(b_all, step, axis=0), 0.0)
            step *= 2
        k_all = (1.0 - lb) * jax.nn.sigmoid(-z)
        q_all = jax.nn.silu(q_ref[rows, :])
        g_all = jax.nn.silu(gate_ref[rows, :])
        for hd in range(heads):
            cols = slice(hd * LANES, (hd + 1) * LANES)
            bc, qc, kc = b_all[:, cols], q_all[:, cols], k_all[:, cols]
            ic = i_ref[rows, cols]
            st = st_ref[hd]
            o = _dot((qc * jnp.exp(bc)).astype(BF16), st.astype(BF16), _NT)
            for s in range(c_len):
                es = jnp.where(trow >= s, jnp.exp(bc - bc[s:s + 1]), 0.0)
                col = jnp.sum(qc * es * kc[s:s + 1], axis=-1, keepdims=True)
                o = o + col * ic[s:s + 1]
            b_last = bc[c_len - 1:c_len]
            kd = kc * jnp.exp(b_last - bc)
            st_ref[hd] = st * jnp.exp(b_last) + _dot(ic.astype(BF16), kd.astype(BF16), _TN)
            y = o * lax.rsqrt(jnp.mean(o * o, axis=-1, keepdims=True) + RMS_EPS) * ng
            o_ref[rows, cols] = (y * g_all[:, cols]).astype(o_ref.dtype)
        return carry

    lax.fori_loop(0, tt // c_len, body, 0)


def hgrn2(z, gamma, norm_g, batch, seq, layer, tt=512, heads=4):
    tt = min(tt, seq)
    nt = seq // tt
    n_even = gamma.shape[0]
    ng = HGRN_HEADS // heads
    width = heads * LANES

    def col(offset):
        return pl.BlockSpec((tt, width), lambda b, h, t: (b * nt + t, offset * ng + h))

    return pl.pallas_call(
        functools.partial(_hgrn_kernel, tt=tt, layer=layer, heads=heads),
        grid=(batch, ng, nt),
        in_specs=[pl.BlockSpec((n_even, width), lambda b, h, t: (0, h)),
                  col(0), col(1), col(2), col(3),
                  pl.BlockSpec((1, LANES), lambda b, h, t: (0, 0))],
        out_specs=pl.BlockSpec((tt, width), lambda b, h, t: (b * nt + t, h)),
        out_shape=jax.ShapeDtypeStruct((batch * seq, HGRN_HEADS * LANES), BF16),
        scratch_shapes=[pltpu.VMEM((heads, LANES, LANES), F32)],
        compiler_params=_params("parallel", "parallel", "arbitrary"),
        name="hgrn2",
    )(gamma, z, z, z, z, norm_g.reshape(1, LANES))


def _topk_rows(vals, k, ids=None):
    r, n = vals.shape
    iota0 = lax.broadcasted_iota(I32, (r, n), 0) if ids is None else ids
    iota_k = lax.broadcasted_iota(I32, (k, n), 0)
    rank = jnp.full((r, n), k, I32)
    tops = []
    stacked = jnp.zeros((k, n), F32)
    for a in range(k):
        m = jnp.max(vals, axis=0, keepdims=True)
        idx = jnp.min(jnp.where(vals == m, iota0, jnp.iinfo(jnp.int32).max), axis=0, keepdims=True)
        hit = iota0 == idx
        rank = jnp.where(hit, a, rank)
        tops.append(m)
        stacked = jnp.where(iota_k == a, m, stacked)
        vals = jnp.where(hit, -jnp.inf, vals)
    return tops, stacked, rank


def _compress_kernel(x_ref, w_ref, pe_ref, o_ref):
    pq = _dot(x_ref[...], w_ref[...])
    nrow = pq.shape[0]
    pe_c = _dot(pe_ref[...], w_ref[...])
    const = pe_c[0:1, :LANES] + pe_c[1:2, LANES:]
    nxt = pltpu.roll(pq[:, LANES:], nrow - 1, axis=0)
    o_ref[...] = (pq[:, :LANES] + nxt + const).astype(o_ref.dtype)


def nsa_compress(xc, wcat, pecat):
    bsz, nj, nrow, kk = xc.shape
    return pl.pallas_call(
        _compress_kernel,
        grid=(bsz, nj),
        in_specs=[pl.BlockSpec((None, None, nrow, kk), lambda b, j: (b, j, 0, 0)),
                  pl.BlockSpec((None, kk, 2 * LANES), lambda b, j: (j // NSA_GROUPS, 0, 0)),
                  pl.BlockSpec((None, 8, kk), lambda b, j: (j // NSA_GROUPS, 0, 0))],
        out_specs=pl.BlockSpec((None, None, nrow, LANES), lambda b, j: (b, j, 0, 0)),
        out_shape=jax.ShapeDtypeStruct((bsz, nj, nrow, LANES), BF16),
        compiler_params=_params("parallel", "parallel"),
        name="nsa_compress",
    )(xc, wcat, pecat)


def _nsa_kernel(q_ref, kc_ref, vc_ref, ks_ref, vs_ref, kw_ref, vw_ref, gt_ref, o_ref,
                m_ref, l_ref, acc_ref, mw_ref, lw_ref, accw_ref, mc_ref, lc_ref, accc_ref,
                psum_ref, sel_ref, *, tq, tk, ncp, nsp, n_top):
    rep = NSA_REP
    q0 = pl.program_id(2) * tq
    qpos = q0 + lax.broadcasted_iota(I32, (tq, 1), 0)

    def q_head(r):
        return q_ref[:, r * LANES:(r + 1) * LANES]

    def once(fn):
        lax.fori_loop(0, jnp.minimum(pl.program_id(2), 0) + 1, lambda _, carry: (fn(), carry)[1], 0)

    has_key = qpos >= CMP_LEN - 1

    def compressed():
        for ref in (m_ref, mw_ref, mc_ref):
            ref[...] = jnp.full(ref.shape, NEG_INF, F32)
        for ref in (l_ref, acc_ref, lw_ref, accw_ref, lc_ref, accc_ref):
            ref[...] = jnp.zeros(ref.shape, F32)
        cend = lax.broadcasted_iota(I32, (1, ncp), 1) * CMP_STRIDE + (CMP_LEN - 1)
        bias_c = jnp.where(cend <= qpos, 0.0, NEG_INF)
        psum_ref[...] = jnp.zeros(psum_ref.shape, F32)
        kc = kc_ref[...]
        _flash_heads([functools.partial(lambda r: _dot(q_head(r), kc, _NT) + bias_c, r)
                      for r in range(rep)], vc_ref[...], mc_ref, lc_ref, accc_ref, psum_ref, has_key)

    once(compressed)

    span = WINDOW + tq
    start = pl.multiple_of(jnp.maximum(q0 - WINDOW, 0), tq)

    def window_and_selection():
        n_i = lax.broadcasted_iota(I32, (ncp, nsp), 0)
        j4 = lax.broadcasted_iota(I32, (ncp, nsp), 1) * (SLC_LEN // CMP_STRIDE)
        overlap = ((n_i >= j4 - (CMP_LEN - 1) // CMP_STRIDE)
                   & (n_i < j4 + SLC_LEN // CMP_STRIDE)).astype(F32)
        imp_t = _dot(psum_ref[...], overlap, precision=lax.Precision.HIGHEST).T
        jt = lax.broadcasted_iota(I32, (nsp, 1), 0)
        cur = (q0 + lax.broadcasted_iota(I32, (1, tq), 1)) // SLC_LEN
        forced = (jt == 0) | (jt == cur) | (jt == cur - 1)
        imp_t = jnp.where(forced, FORCE_SCORE, jnp.where(jt <= cur, imp_t, -1.0))
        _, _, rank = _topk_rows(imp_t, n_top)
        sel_ref[...] = (rank < n_top).astype(F32).T.astype(BF16)

        kpos = start + lax.broadcasted_iota(I32, (1, span), 1)
        bias_w = jnp.where((kpos <= qpos) & (kpos > qpos - WINDOW), 0.0, NEG_INF)
        k_w = kw_ref[pl.ds(start, span), :]
        _flash_heads([functools.partial(lambda r: _dot(q_head(r), k_w, _NT) + bias_w, r)
                      for r in range(rep)], vw_ref[pl.ds(start, span), :], mw_ref, lw_ref, accw_ref)

    once(window_and_selection)

    jj = lax.broadcasted_iota(I32, (nsp, 1), 0)

    def body(c, carry):
        k0 = pl.multiple_of(c * tk, tk)
        k = ks_ref[pl.ds(k0, tk), :]
        v = vs_ref[pl.ds(k0, tk), :]
        kpos = k0 + lax.broadcasted_iota(I32, (1, tk), 1)
        expand = (jj == kpos // SLC_LEN).astype(BF16)
        allowed = (_dot(sel_ref[...], expand) > 0.5) & (kpos <= qpos)
        bias = jnp.where(allowed, 0.0, NEG_INF)
        _flash_heads([functools.partial(lambda r: _dot(q_head(r), k, _NT) + bias, r)
                      for r in range(rep)], v, m_ref, l_ref, acc_ref)
        return carry

    lax.fori_loop(0, (q0 + tq + tk - 1) // tk, body, 0)

    gt = gt_ref[...]
    for r in range(rep):
        o = (gt[:, 3 * r:3 * r + 1] * jnp.where(has_key, accc_ref[r] / lc_ref[r], 0.0)
             + gt[:, 3 * r + 1:3 * r + 2] * (acc_ref[r] / l_ref[r])
             + gt[:, 3 * r + 2:3 * r + 3] * (accw_ref[r] / lw_ref[r]))
        o_ref[:, r * LANES:(r + 1) * LANES] = o.astype(o_ref.dtype)


def nsa_attention(z, kvc, gates, batch, seq, tq=128, tk=1024):
    tk = min(tk, seq)
    assert seq % tq == 0 and seq >= WINDOW + tq and tq % SLC_LEN == 0 and seq % tk == 0
    nq = seq // tq
    ncp = seq // CMP_STRIDE
    n_slc = seq // SLC_LEN
    nsp = -(-n_slc // LANES) * LANES
    n_top = min(SLC_TOPN, n_slc)
    qw = NSA_REP * LANES
    g = NSA_GROUPS
    q_cols = g * NSA_REP

    def kv_col(which):
        return pl.BlockSpec((seq, LANES), lambda b, gi, i: (b, q_cols + which * g + gi))

    head_state = pltpu.VMEM((NSA_REP, tq, LANES), F32)
    return pl.pallas_call(
        functools.partial(_nsa_kernel, tq=tq, tk=tk, ncp=ncp, nsp=nsp, n_top=n_top),
        grid=(batch, g, nq),
        in_specs=[pl.BlockSpec((tq, qw), lambda b, gi, i: (b * nq + i, gi)),
                  pl.BlockSpec((None, None, ncp, LANES), lambda b, gi, i: (b, gi, 0, 0)),
                  pl.BlockSpec((None, None, ncp, LANES), lambda b, gi, i: (b, g + gi, 0, 0)),
                  kv_col(2), kv_col(3), kv_col(4), kv_col(5),
                  pl.BlockSpec((tq, LANES), lambda b, gi, i: (b * nq + i, gi))],
        out_specs=pl.BlockSpec((tq, qw), lambda b, gi, i: (b * nq + i, gi)),
        out_shape=jax.ShapeDtypeStruct((batch * seq, g * qw), BF16),
        scratch_shapes=[head_state] * 9 + [pltpu.VMEM((tq, ncp), F32), pltpu.VMEM((tq, nsp), BF16)],
        compiler_params=_params("parallel", "parallel", "arbitrary"),
        name="nsa_attention",
    )(z, kvc, kvc, z, z, z, z, gates)


def _xattn_kernel(hb_ref, h_ref, wq_ref, kv_ref, wo_ref, g_ref, b_ref, o_ref, ob_ref, *, alpha):
    width = XATTN_HEADS * LANES
    scale = LANES ** -0.5
    q = _dot(hb_ref[...], wq_ref[...]).astype(BF16)
    kv = kv_ref[...]
    outs = []
    for hd in range(XATTN_HEADS):
        sl = slice(hd * LANES, (hd + 1) * LANES)
        s = _dot(q[:, sl], kv[:, sl], _NT) * scale
        e = jnp.exp(s - jnp.max(s, axis=-1, keepdims=True))
        p = e / jnp.sum(e, axis=-1, keepdims=True)
        outs.append(_dot(p.astype(BF16), kv[:, width + hd * LANES:width + (hd + 1) * LANES]))
    o = jnp.concatenate(outs, axis=1).astype(BF16)
    y = _dot(o, wo_ref[...])
    out = _layer_norm_rows(alpha * h_ref[...] + y, g_ref[...], b_ref[...])
    o_ref[...] = out
    ob_ref[...] = out.astype(BF16)


def xattn_block(hb, h, wq, kv, wo, g, b, alpha, batch, seq, tm=256):
    n, d = h.shape
    tm = min(tm, seq)
    nt = seq // tm
    mem_len = kv.shape[0] // batch
    width = XATTN_HEADS * LANES
    return pl.pallas_call(
        functools.partial(_xattn_kernel, alpha=alpha),
        grid=(batch, nt),
        in_specs=[pl.BlockSpec((tm, d), lambda bi, i: (bi * nt + i, 0)),
                  pl.BlockSpec((tm, d), lambda bi, i: (bi * nt + i, 0)),
                  pl.BlockSpec((d, width), lambda bi, i: (0, 0)),
                  pl.BlockSpec((mem_len, 2 * width), lambda bi, i: (bi, 0)),
                  pl.BlockSpec((width, d), lambda bi, i: (0, 0)),
                  pl.BlockSpec((1, d), lambda bi, i: (0, 0)),
                  pl.BlockSpec((1, d), lambda bi, i: (0, 0))],
        out_specs=[pl.BlockSpec((tm, d), lambda bi, i: (bi * nt + i, 0)),
                   pl.BlockSpec((tm, d), lambda bi, i: (bi * nt + i, 0))],
        out_shape=[jax.ShapeDtypeStruct((n, d), F32), jax.ShapeDtypeStruct((n, d), BF16)],
        compiler_params=_params("parallel", "parallel"),
        name="xattn_block",
    )(hb, h, wq, kv, wo, g.reshape(1, d), b.reshape(1, d))


def _peer_route_kernel(q_ref, keys_ref, mb_ref, e1_ref, b2_ref, e2_ref, *, tn):
    k = PEER_TOPK
    nk = PEER_N_KEYS
    hi = lax.Precision.HIGHEST
    for h in range(PEER_HEADS):
        s1 = _dot(keys_ref[h, 0], q_ref[:, (2 * h) * LANES:(2 * h + 1) * LANES], _NT, hi)
        s2 = _dot(keys_ref[h, 1], q_ref[:, (2 * h + 1) * LANES:(2 * h + 2) * LANES], _NT, hi)
        v1, v1_all, rank1 = _topk_rows(s1, k)
        v2, v2_all, rank2 = _topk_rows(s2, k)
        half = k // 2
        iota_a = lax.broadcasted_iota(I32, (k, tn), 0)
        is_hi = iota_a >= half
        cand = [v1_all + v2[b] for b in range(half)] + [jnp.where(is_hi, v1[0] + v2_all, -jnp.inf)]
        ids = [iota_a * k + b for b in range(half)] + [jnp.where(is_hi, iota_a, iota_a + k * k)]
        tops, _, crank = _topk_rows(jnp.concatenate(cand, axis=0), k, jnp.concatenate(ids, axis=0))
        zsum = tops[0] * 0.0
        for a in range(k):
            zsum = zsum + jnp.exp(tops[a] - tops[0])
        cnt = jnp.zeros((k, tn), F32)
        for b in range(half):
            cnt = cnt + jnp.where(crank[b * k:(b + 1) * k] < k, 1.0, 0.0)
        won_hi = (crank[half * k:(half + 1) * k] < k) & is_hi
        extra = jnp.sum(jnp.where(won_hi, 1.0, 0.0), axis=0, keepdims=True)
        cnt = (cnt + jnp.where(iota_a == 0, extra, 0.0)).astype(I32)
        mb = jnp.zeros((nk, tn), I32)
        for a in range(k):
            mb = jnp.where(rank1 == a, cnt[a:a + 1], mb)
        mb_ref[h] = mb
        b2_ref[h] = rank2.astype(b2_ref.dtype)
        e1_ref[h] = jnp.where(rank1 < k, jnp.exp(s1 - v1[0]), 0.0) / zsum
        e2_ref[h] = jnp.where(rank2 < k, jnp.exp(s2 - v2[0]), 0.0).astype(e2_ref.dtype)


def peer_route(q, keys, tn=256):
    n = q.shape[0]
    tn = min(tn, n)
    tab = lambda dt: jax.ShapeDtypeStruct((PEER_HEADS, PEER_N_KEYS, n), dt)
    tspec = pl.BlockSpec((PEER_HEADS, PEER_N_KEYS, tn), lambda i: (0, 0, i))
    return pl.pallas_call(
        functools.partial(_peer_route_kernel, tn=tn),
        grid=(n // tn,),
        in_specs=[pl.BlockSpec((tn, q.shape[1]), lambda i: (i, 0)),
                  pl.BlockSpec(keys.shape, lambda i: (0, 0, 0, 0))],
        out_specs=[tspec, tspec, tspec, tspec],
        out_shape=[tab(I32), tab(F32), tab(jnp.int16), tab(BF16)],
        compiler_params=_params("parallel"),
        name="peer_route",
    )(q, keys)


def _peer_dense_kernel(x_ref, u_ref, vt_ref, mb_ref, e1_ref, b2_ref, e2_ref, o_ref, w_ref,
                       *, te, tn):
    j = pl.program_id(1)

    @pl.when(j == 0)
    def _():
        o_ref[...] = jnp.zeros(o_ref.shape, F32)

    nk = PEER_N_KEYS
    for sub in range(te // nk):
        i1 = j * (te // nk) + sub
        w = jnp.zeros((nk, tn), BF16)
        for h in range(PEER_HEADS):
            hit = b2_ref[h] < mb_ref[h, pl.ds(i1, 1), :].astype(jnp.int16)
            gate = e1_ref[h, pl.ds(i1, 1), :].astype(BF16) * e2_ref[h]
            w = w + jnp.where(hit, gate, jnp.zeros((), BF16))
        w_ref[sub * nk:(sub + 1) * nk, :] = w.astype(F32)

    chains = list(range(0, te, PEER_CHAIN))
    hts = [_dot(u_ref[c0:c0 + PEER_CHAIN, :], x_ref[...], _NT) for c0 in chains]
    for c0, ht in zip(chains, hts):
        gelu = 0.5 * ht * (1.0 + lax.erf(ht * (2.0 ** -0.5)))
        a = (gelu * w_ref[c0:c0 + PEER_CHAIN, :]).astype(BF16)
        o_ref[...] += _dot(vt_ref[:, c0:c0 + PEER_CHAIN], a)


def peer_dense(xb, u, vt, tables, tn=512, te=512):
    n, d = xb.shape
    n_exp = u.shape[0]
    tn = min(tn, n)
    assert te % PEER_CHAIN == 0 and n_exp % te == 0
    once = pl.Buffered(1)
    tspec = pl.BlockSpec((PEER_HEADS, PEER_N_KEYS, tn), lambda i, j: (0, 0, i), pipeline_mode=once)
    return pl.pallas_call(
        functools.partial(_peer_dense_kernel, te=te, tn=tn),
        grid=(n // tn, n_exp // te),
        in_specs=[pl.BlockSpec((tn, d), lambda i, j: (i, 0), pipeline_mode=once),
                  pl.BlockSpec((te, d), lambda i, j: (j, 0)),
                  pl.BlockSpec((d, te), lambda i, j: (0, j)),
                  tspec, tspec, tspec, tspec],
        out_specs=pl.BlockSpec((d, tn), lambda i, j: (0, i)),
        out_shape=jax.ShapeDtypeStruct((d, n), F32),
        scratch_shapes=[pltpu.VMEM((te, tn), F32)],
        compiler_params=_params("parallel", "arbitrary"),
        name="peer_dense",
    )(xb, u, vt, *tables)


def odd_mixer_core(hb, w_in, cmp_pe, cmp_w, batch, seq, nsa_tq=128):
    q_w = NSA_GROUPS * NSA_REP * LANES
    kv_w = NSA_GROUPS * LANES
    nsa_main = q_w + 6 * kv_w
    n_gate = 3 * NSA_GROUPS * NSA_REP
    z = matmul(hb, w_in[:, :nsa_main].astype(BF16), BF16, scaled_cols=q_w, scale=QK_LOG2_SCALE)
    per_group = n_gate // NSA_GROUPS
    w_gate = w_in[:, nsa_main:].reshape(-1, NSA_GROUPS, per_group)
    w_gate = jnp.pad(w_gate, ((0, 0), (0, 0), (0, LANES - per_group)))
    w_gate = w_gate.reshape(-1, NSA_GROUPS * LANES).astype(BF16)
    gates = matmul(hb, w_gate, F32, act="sigmoid")
    xc = z[:, q_w:q_w + 2 * kv_w].reshape(batch, seq // CMP_STRIDE, CMP_STRIDE, 2 * NSA_GROUPS, LANES)
    xc = xc.transpose(0, 3, 1, 2, 4).reshape(
        batch, 2 * NSA_GROUPS, seq // CMP_STRIDE, CMP_STRIDE * LANES)
    half = CMP_STRIDE * LANES
    cw = cmp_w.astype(BF16)
    wcat = jnp.concatenate([cw[:, :half], cw[:, half:]], axis=2)
    pe = cmp_pe.reshape(2, 2, 1, half).astype(BF16)
    pecat = jnp.concatenate([pe[:, 0], pe[:, 1], jnp.zeros((2, 6, half), BF16)], axis=1)
    kvc = nsa_compress(xc, wcat, pecat)
    return nsa_attention(z, kvc, gates, batch, seq, tq=nsa_tq)


def even_mixer_core(hb, w_in, lam_vec, diff_g, gamma, hgrn_g, batch, seq, lam_init, layer):
    diff_w = DIFF_HEADS * 2 * LANES
    attn_cols = 3 * diff_w
    w_in = w_in.astype(BF16)
    z_attn = matmul(hb, w_in[:, :attn_cols], BF16, scaled_cols=diff_w, scale=QK_LOG2_SCALE)
    z_rec = matmul(hb, w_in[:, attn_cols:], F32)
    o_a = diff_attention(z_attn, lam_vec, diff_g, batch, seq, lam_init)
    o_b = hgrn2(z_rec, gamma, hgrn_g, batch, seq, layer)
    return jnp.concatenate([o_a, o_b], axis=1)


def peer_core(hb, wq, keys, u, v):
    pq = matmul(hb, wq.astype(BF16), F32)
    tables = peer_route(pq, keys)
    return peer_dense(hb, u.astype(BF16), v.T.astype(BF16), tables)


def kernel(x, mem, w_in_even, w_out_even, diff_lambda, diff_norm_g, hgrn_gamma, hgrn_norm_g,
           w_in_odd, w_out_odd, cmp_pe, cmp_w, xattn_wq, xattn_wkv, xattn_wo, ln_g, ln_b,
           peer_wq, peer_keys, peer_u, peer_v):
    batch, seq, d = x.shape
    depth = ln_g.shape[0]
    n = batch * seq
    assert d == 32 * LANES, "head dims are d_model/32 and must equal the lane count"
    alpha = (2.0 * depth) ** 0.25

    h = x.reshape(n, d)
    hb = h.astype(BF16)
    memb = mem.reshape(-1, d).astype(BF16)

    for l in range(depth):
        j = l // 2
        if l % 2 == 0:
            lam_init = 0.8 - 0.6 * math.exp(-0.3 * l)
            mix = even_mixer_core(hb, w_in_even[j], diff_lambda[j], diff_norm_g[j], hgrn_gamma,
                                  hgrn_norm_g[j], batch, seq, lam_init, j)
            y = matmul(mix, w_out_even[j].astype(BF16), F32)
        else:
            o = odd_mixer_core(hb, w_in_odd[j], cmp_pe[j], cmp_w[j], batch, seq)
            y = matmul(o, w_out_odd[j].astype(BF16), F32)
        h, hb = ln_residual(h, y, ln_g[l, 0], ln_b[l, 0], alpha)

        kv = matmul(memb, xattn_wkv[l].astype(BF16), BF16)
        h, hb = xattn_block(hb, h, xattn_wq[l].astype(BF16), kv, xattn_wo[l].astype(BF16),
                            ln_g[l, 1], ln_b[l, 1], alpha, batch, seq)

        yt = peer_core(hb, peer_wq[l], peer_keys[l], peer_u[l], peer_v[l])
        h, hb = ln_residual(h, yt, ln_g[l, 2], ln_b[l, 2], alpha, y_transposed=True)

    return h.reshape(batch, seq, d)
```

```python
import functools
import math

import jax
import jax.numpy as jnp
from jax import lax
from jax.experimental import pallas as pl
from jax.experimental.pallas import tpu as pltpu

F32 = jnp.float32
BF16 = jnp.bfloat16
I32 = jnp.int32

LANES = 128
VMEM_LIMIT_BYTES = 56 * 1024 * 1024

DIFF_HEADS = 8
HGRN_HEADS = 16
HGRN_CHUNK = 16
LB_FLOOR = 1e-30
NSA_GROUPS = 4
NSA_REP = 8
CMP_LEN = 32
CMP_STRIDE = 16
SLC_LEN = 64
SLC_TOPN = 16
WINDOW = 512
FORCE_SCORE = 1e9
XATTN_HEADS = 4
PEER_HEADS = 8
PEER_N_KEYS = 128
PEER_TOPK = 16
FLASH_LOOKAHEAD = 3
PEER_CHAIN = 256
QK_LOG2_SCALE = LANES ** -0.5 * math.log2(math.e)
LN_EPS = 1e-5
RMS_EPS = 1e-6
NEG_INF = -1e30

_NT = (((1,), (1,)), ((), ()))
_TN = (((0,), (0,)), ((), ()))


def _params(*sem):
    return pltpu.CompilerParams(dimension_semantics=sem, vmem_limit_bytes=VMEM_LIMIT_BYTES)


def _dot(a, b, dims=None, precision=None):
    if dims is None:
        return jnp.dot(a, b, preferred_element_type=F32, precision=precision)
    return lax.dot_general(a, b, dims, preferred_element_type=F32, precision=precision)


def _matmul_kernel(x_ref, w_ref, o_ref, *, act, scaled_blocks, scale):
    acc = _dot(x_ref[...], w_ref[...])
    if scaled_blocks:
        acc = acc * jnp.where(pl.program_id(1) < scaled_blocks, scale, 1.0)
    if act == "sigmoid":
        acc = jax.nn.sigmoid(acc)
    o_ref[...] = acc.astype(o_ref.dtype)


def matmul(x, w, out_dtype, act=None, scaled_cols=0, scale=1.0, tm=1024, tn=1024):
    m, k = x.shape
    n = w.shape[1]
    tm = min(tm, m)
    tn = min(tn, n)
    assert m % tm == 0 and n % tn == 0 and scaled_cols % tn == 0, (m, n, tm, tn, scaled_cols)
    return pl.pallas_call(
        functools.partial(_matmul_kernel, act=act, scaled_blocks=scaled_cols // tn, scale=scale),
        grid=(m // tm, n // tn),
        in_specs=[pl.BlockSpec((tm, k), lambda i, j: (i, 0)),
                  pl.BlockSpec((k, tn), lambda i, j: (0, j))],
        out_specs=pl.BlockSpec((tm, tn), lambda i, j: (i, j)),
        out_shape=jax.ShapeDtypeStruct((m, n), out_dtype),
        compiler_params=_params("parallel", "parallel"),
        name="matmul",
    )(x, w)


def _layer_norm_rows(z, g, b):
    mu = jnp.mean(z, axis=-1, keepdims=True)
    zc = z - mu
    var = jnp.mean(zc * zc, axis=-1, keepdims=True)
    return zc * lax.rsqrt(var + LN_EPS) * g + b


def _ln_res_kernel(h_ref, y_ref, g_ref, b_ref, o_ref, ob_ref, *, alpha, y_transposed):
    y = y_ref[...]
    if y_transposed:
        y = y.T
    out = _layer_norm_rows(alpha * h_ref[...] + y, g_ref[...], b_ref[...])
    o_ref[...] = out
    ob_ref[...] = out.astype(BF16)


def ln_residual(h, y, g, b, alpha, y_transposed=False, tm=256):
    n, d = h.shape
    tm = min(tm, n)
    y_spec = (pl.BlockSpec((d, tm), lambda i: (0, i)) if y_transposed
              else pl.BlockSpec((tm, d), lambda i: (i, 0)))
    return pl.pallas_call(
        functools.partial(_ln_res_kernel, alpha=alpha, y_transposed=y_transposed),
        grid=(n // tm,),
        in_specs=[pl.BlockSpec((tm, d), lambda i: (i, 0)), y_spec,
                  pl.BlockSpec((1, d), lambda i: (0, 0)),
                  pl.BlockSpec((1, d), lambda i: (0, 0))],
        out_specs=[pl.BlockSpec((tm, d), lambda i: (i, 0)),
                   pl.BlockSpec((tm, d), lambda i: (i, 0))],
        out_shape=[jax.ShapeDtypeStruct((n, d), F32), jax.ShapeDtypeStruct((n, d), BF16)],
        compiler_params=_params("parallel"),
        name="ln_residual",
    )(h, y, g.reshape(1, d), b.reshape(1, d))


def _lane_tile(x, width):
    reps = width // LANES
    return x if reps == 1 else jnp.concatenate([x] * reps, axis=1)


def _flash_heads(score_fns, v, m_ref, l_ref, acc_ref, psum_ref=None, row_ok=None, states=None):
    n = len(score_fns)
    dv = acc_ref.shape[-1]
    states = list(range(n)) if states is None else states
    values = v if isinstance(v, (list, tuple)) else [v] * n
    ahead = min(FLASH_LOOKAHEAD, n)
    scores = [score_fns[r]() for r in range(ahead)]
    pending = []

    def retire():
        pr, pa, ppv = pending.pop(0)
        acc_ref[pr] = _lane_tile(pa, dv) * acc_ref[pr] + ppv

    for r in range(n):
        s = scores.pop(0)
        if r + ahead < n:
            scores.append(score_fns[r + ahead]())
        st = states[r]
        m_prev = m_ref[st]
        m_next = jnp.maximum(m_prev, jnp.max(s, axis=-1, keepdims=True))
        p = jnp.exp2(s - _lane_tile(m_next, s.shape[1]))
        alpha = jnp.exp2(m_prev - m_next)
        l_new = alpha * l_ref[st] + jnp.sum(p, axis=-1, keepdims=True)
        l_ref[st] = l_new
        m_ref[st] = m_next
        if psum_ref is not None:
            psum_ref[...] += p * _lane_tile(jnp.where(row_ok, 1.0 / l_new, 0.0), s.shape[1])
        if len(pending) == ahead:
            retire()
        pending.append((st, alpha, _dot(p.astype(BF16), values[r])))
    while pending:
        retire()


def _diff_attn_kernel(lam_ref, q_ref, k_ref, v_ref, g_ref, o_ref, m_ref, l_ref, acc_ref,
                      *, tq, tk, lam_init):
    qi = pl.program_id(2)
    m_ref[...] = jnp.full(m_ref.shape, NEG_INF, F32)
    l_ref[...] = jnp.zeros(l_ref.shape, F32)
    acc_ref[...] = jnp.zeros(acc_ref.shape, F32)

    def units(c, bias):
        k0 = pl.multiple_of(c * tk, tk)

        def scores(half):
            sl = slice(half * LANES, (half + 1) * LANES)
            s = _dot(q_ref[:, sl], k_ref[pl.ds(k0, tk), sl], _NT)
            return s if bias is None else s + bias

        return [functools.partial(scores, half) for half in range(2)], v_ref[pl.ds(k0, tk), :]

    def run(chunks):
        fns, vals = [], []
        for c, bias in chunks:
            f, v = units(c, bias)
            fns += f
            vals += [v, v]
        _flash_heads(fns, vals, m_ref, l_ref, acc_ref, states=[0, 1] * len(chunks))

    def body(cp, carry):
        run([(2 * cp, None), (2 * cp + 1, None)])
        return carry

    lax.fori_loop(0, qi // 2, body, 0)
    causal = (lax.broadcasted_iota(I32, (1, tk), 1) <= lax.broadcasted_iota(I32, (tq, 1), 0))
    diag_bias = jnp.where(causal, 0.0, NEG_INF)

    @pl.when(qi % 2 == 1)
    def _():
        run([(qi - 1, None), (qi, diag_bias)])

    @pl.when(qi % 2 == 0)
    def _():
        run([(qi, diag_bias)])


    lv = lam_ref[...]
    lam = (jnp.exp(jnp.sum(lv[0:1] * lv[1:2], axis=-1, keepdims=True))
           - jnp.exp(jnp.sum(lv[2:3] * lv[3:4], axis=-1, keepdims=True)) + lam_init)
    dv = acc_ref.shape[-1]
    o = (acc_ref[0] / _lane_tile(l_ref[0], dv) - lam * (acc_ref[1] / _lane_tile(l_ref[1], dv)))
    o = o * lax.rsqrt(jnp.mean(o * o, axis=-1, keepdims=True) + RMS_EPS)
    o_ref[...] = (o * g_ref[...] * (1.0 - lam_init)).astype(o_ref.dtype)


def diff_attention(z, lam_vec, norm_g, batch, seq, lam_init, tq=512):
    dv = 2 * LANES
    tq = tk = min(tq, seq)
    nq = seq // tq
    return pl.pallas_call(
        functools.partial(_diff_attn_kernel, tq=tq, tk=tk, lam_init=lam_init),
        grid=(batch, DIFF_HEADS, nq),
        in_specs=[pl.BlockSpec((4, LANES), lambda b, h, i: (0, 0)),
                  pl.BlockSpec((tq, dv), lambda b, h, i: (b * nq + i, h)),
                  pl.BlockSpec((seq, dv), lambda b, h, i: (b, DIFF_HEADS + h)),
                  pl.BlockSpec((seq, dv), lambda b, h, i: (b, 2 * DIFF_HEADS + h)),
                  pl.BlockSpec((1, dv), lambda b, h, i: (0, 0))],
        out_specs=pl.BlockSpec((tq, dv), lambda b, h, i: (b * nq + i, h)),
        out_shape=jax.ShapeDtypeStruct((batch * seq, DIFF_HEADS * dv), BF16),
        scratch_shapes=[pltpu.VMEM((2, tq, LANES), F32), pltpu.VMEM((2, tq, LANES), F32),
                        pltpu.VMEM((2, tq, dv), F32)],
        compiler_params=_params("parallel", "parallel", "arbitrary"),
        name="diff_attention",
    )(lam_vec, z, z, z, norm_g.reshape(1, dv))


def _hgrn_kernel(gam_ref, q_ref, f_ref, i_ref, gate_ref, ng_ref, o_ref,
                 st_ref, *, tt, layer, heads):
    c_len = HGRN_CHUNK
    width = heads * LANES

    @pl.when(pl.program_id(2) == 0)
    def _():
        st_ref[...] = jnp.zeros(st_ref.shape, F32)

    gam = gam_ref[...]
    e = jnp.exp(gam - jnp.max(gam, axis=0, keepdims=True))
    p = e / jnp.sum(e, axis=0, keepdims=True)
    lb = jnp.zeros((1, width), F32)
    for r in range(1, layer + 1):
        lb = lb + p[r:r + 1]
    log_lb = jnp.log(jnp.maximum(lb, LB_FLOOR))
    log_1m = jnp.log1p(-lb)
    trow = lax.broadcasted_iota(I32, (c_len, 1), 0)
    ng = ng_ref[...]

    def body(c, carry):
        rows = pl.ds(pl.multiple_of(c * c_len, c_len), c_len)
        z = f_ref[rows, :]
        log_sig = jnp.minimum(z, 0.0) - jnp.log1p(jnp.exp(-jnp.abs(z)))
        log_gate = log_1m + log_sig
        b_all = (jnp.maximum(log_lb, log_gate)
                 + jnp.log1p(jnp.exp(-jnp.abs(log_lb - log_gate))))
        step = 1
        while step < c_len:
            b_all = b_all + jnp.where(trow >= step, pltpu.roll(b_all, step, axis=0), 0.0)
            step *= 2
        k_all = (1.0 - lb) * jax.nn.sigmoid(-z)
        q_all = jax.nn.silu(q_ref[rows, :])
        g_all = jax.nn.silu(gate_ref[rows, :])
        for hd in range(heads):
            cols = slice(hd * LANES, (hd + 1) * LANES)
            bc, qc, kc = b_all[:, cols], q_all[:, cols], k_all[:, cols]
            ic = i_ref[rows, cols]
            st = st_ref[hd]
            o = _dot((qc * jnp.exp(bc)).astype(BF16), st.astype(BF16), _NT)
            for s in range(c_len):
                es = jnp.where(trow >= s, jnp.exp(bc - bc[s:s + 1]), 0.0)
                col = jnp.sum(qc * es * kc[s:s + 1], axis=-1, keepdims=True)
                o = o + col * ic[s:s + 1]
            b_last = bc[c_len - 1:c_len]
            kd = kc * jnp.exp(b_last - bc)
            st_ref[hd] = st * jnp.exp(b_last) + _dot(ic.astype(BF16), kd.astype(BF16), _TN)
            y = o * lax.rsqrt(jnp.mean(o * o, axis=-1, keepdims=True) + RMS_EPS) * ng
            o_ref[rows, cols] = (y * g_all[:, cols]).astype(o_ref.dtype)
        return carry

    lax.fori_loop(0, tt // c_len, body, 0)


def hgrn2(z, gamma, norm_g, batch, seq, layer, tt=512, heads=4):
    tt = min(tt, seq)
    nt = seq // tt
    n_even = gamma.shape[0]
    ng = HGRN_HEADS // heads
    width = heads * LANES

    def col(offset):
        return pl.BlockSpec((tt, width), lambda b, h, t: (b * nt + t, offset * ng + h))

    return pl.pallas_call(
        functools.partial(_hgrn_kernel, tt=tt, layer=layer, heads=heads),
        grid=(batch, ng, nt),
        in_specs=[pl.BlockSpec((n_even, width), lambda b, h, t: (0, h)),
                  col(0), col(1), col(2), col(3),
                  pl.BlockSpec((1, LANES), lambda b, h, t: (0, 0))],
        out_specs=pl.BlockSpec((tt, width), lambda b, h, t: (b * nt + t, h)),
        out_shape=jax.ShapeDtypeStruct((batch * seq, HGRN_HEADS * LANES), BF16),
        scratch_shapes=[pltpu.VMEM((heads, LANES, LANES), F32)],
        compiler_params=_params("parallel", "parallel", "arbitrary"),
        name="hgrn2",
    )(gamma, z, z, z, z, norm_g.reshape(1, LANES))


def _topk_rows(vals, k, ids=None):
    r, n = vals.shape
    iota0 = lax.broadcasted_iota(I32, (r, n), 0) if ids is None else ids
    iota_k = lax.broadcasted_iota(I32, (k, n), 0)
    rank = jnp.full((r, n), k, I32)
    tops = []
    stacked = jnp.zeros((k, n), F32)
    for a in range(k):
        m = jnp.max(vals, axis=0, keepdims=True)
        idx = jnp.min(jnp.where(vals == m, iota0, jnp.iinfo(jnp.int32).max), axis=0, keepdims=True)
        hit = iota0 == idx
        rank = jnp.where(hit, a, rank)
        tops.append(m)
        stacked = jnp.where(iota_k == a, m, stacked)
        vals = jnp.where(hit, -jnp.inf, vals)
    return tops, stacked, rank


def _compress_kernel(x_ref, w_ref, pe_ref, o_ref):
    pq = _dot(x_ref[...], w_ref[...])
    nrow = pq.shape[0]
    pe_c = _dot(pe_ref[...], w_ref[...])
    const = pe_c[0:1, :LANES] + pe_c[1:2, LANES:]
    nxt = pltpu.roll(pq[:, LANES:], nrow - 1, axis=0)
    o_ref[...] = (pq[:, :LANES] + nxt + const).astype(o_ref.dtype)


def nsa_compress(xc, wcat, pecat):
    bsz, nj, nrow, kk = xc.shape
    return pl.pallas_call(
        _compress_kernel,
        grid=(bsz, nj),
        in_specs=[pl.BlockSpec((None, None, nrow, kk), lambda b, j: (b, j, 0, 0)),
                  pl.BlockSpec((None, kk, 2 * LANES), lambda b, j: (j // NSA_GROUPS, 0, 0)),
                  pl.BlockSpec((None, 8, kk), lambda b, j: (j // NSA_GROUPS, 0, 0))],
        out_specs=pl.BlockSpec((None, None, nrow, LANES), lambda b, j: (b, j, 0, 0)),
        out_shape=jax.ShapeDtypeStruct((bsz, nj, nrow, LANES), BF16),
        compiler_params=_params("parallel", "parallel"),
        name="nsa_compress",
    )(xc, wcat, pecat)


def _nsa_kernel(q_ref, kc_ref, vc_ref, ks_ref, vs_ref, kw_ref, vw_ref, gt_ref, o_ref,
                m_ref, l_ref, acc_ref, mw_ref, lw_ref, accw_ref, mc_ref, lc_ref, accc_ref,
                psum_ref, sel_ref, *, tq, tk, ncp, nsp, n_top):
    rep = NSA_REP
    q0 = pl.program_id(2) * tq
    qpos = q0 + lax.broadcasted_iota(I32, (tq, 1), 0)

    def q_head(r):
        return q_ref[:, r * LANES:(r + 1) * LANES]

    def once(fn):
        lax.fori_loop(0, jnp.minimum(pl.program_id(2), 0) + 1, lambda _, carry: (fn(), carry)[1], 0)

    has_key = qpos >= CMP_LEN - 1

    def compressed():
        for ref in (m_ref, mw_ref, mc_ref):
            ref[...] = jnp.full(ref.shape, NEG_INF, F32)
        for ref in (l_ref, acc_ref, lw_ref, accw_ref, lc_ref, accc_ref):
            ref[...] = jnp.zeros(ref.shape, F32)
        cend = lax.broadcasted_iota(I32, (1, ncp), 1) * CMP_STRIDE + (CMP_LEN - 1)
        bias_c = jnp.where(cend <= qpos, 0.0, NEG_INF)
        psum_ref[...] = jnp.zeros(psum_ref.shape, F32)
        kc = kc_ref[...]
        _flash_heads([functools.partial(lambda r: _dot(q_head(r), kc, _NT) + bias_c, r)
                      for r in range(rep)], vc_ref[...], mc_ref, lc_ref, accc_ref, psum_ref, has_key)

    once(compressed)

    span = WINDOW + tq
    start = pl.multiple_of(jnp.maximum(q0 - WINDOW, 0), tq)

    def window_and_selection():
        n_i = lax.broadcasted_iota(I32, (ncp, nsp), 0)
        j4 = lax.broadcasted_iota(I32, (ncp, nsp), 1) * (SLC_LEN // CMP_STRIDE)
        overlap = ((n_i >= j4 - (CMP_LEN - 1) // CMP_STRIDE)
                   & (n_i < j4 + SLC_LEN // CMP_STRIDE)).astype(F32)
        imp_t = _dot(psum_ref[...], overlap, precision=lax.Precision.HIGHEST).T
        jt = lax.broadcasted_iota(I32, (nsp, 1), 0)
        cur = (q0 + lax.broadcasted_iota(I32, (1, tq), 1)) // SLC_LEN
        forced = (jt == 0) | (jt == cur) | (jt == cur - 1)
        imp_t = jnp.where(forced, FORCE_SCORE, jnp.where(jt <= cur, imp_t, -1.0))
        _, _, rank = _topk_rows(imp_t, n_top)
        sel_ref[...] = (rank < n_top).astype(F32).T.astype(BF16)

        kpos = start + lax.broadcasted_iota(I32, (1, span), 1)
        bias_w = jnp.where((kpos <= qpos) & (kpos > qpos - WINDOW), 0.0, NEG_INF)
        k_w = kw_ref[pl.ds(start, span), :]
        _flash_heads([functools.partial(lambda r: _dot(q_head(r), k_w, _NT) + bias_w, r)
                      for r in range(rep)], vw_ref[pl.ds(start, span), :], mw_ref, lw_ref, accw_ref)

    once(window_and_selection)

    jj = lax.broadcasted_iota(I32, (nsp, 1), 0)

    def body(c, carry):
        k0 = pl.multiple_of(c * tk, tk)
        k = ks_ref[pl.ds(k0, tk), :]
        v = vs_ref[pl.ds(k0, tk), :]
        kpos = k0 + lax.broadcasted_iota(I32, (1, tk), 1)
        expand = (jj == kpos // SLC_LEN).astype(BF16)
        allowed = (_dot(sel_ref[...], expand) > 0.5) & (kpos <= qpos)
        bias = jnp.where(allowed, 0.0, NEG_INF)
        _flash_heads([functools.partial(lambda r: _dot(q_head(r), k, _NT) + bias, r)
                      for r in range(rep)], v, m_ref, l_ref, acc_ref)
        return carry

    lax.fori_loop(0, (q0 + tq + tk - 1) // tk, body, 0)

    gt = gt_ref[...]
    for r in range(rep):
        o = (gt[:, 3 * r:3 * r + 1] * jnp.where(has_key, accc_ref[r] / lc_ref[r], 0.0)
             + gt[:, 3 * r + 1:3 * r + 2] * (acc_ref[r] / l_ref[r])
             + gt[:, 3 * r + 2:3 * r + 3] * (accw_ref[r] / lw_ref[r]))
        o_ref[:, r * LANES:(r + 1) * LANES] = o.astype(o_ref.dtype)


def nsa_attention(z, kvc, gates, batch, seq, tq=128, tk=1024):
    tk = min(tk, seq)
    assert seq % tq == 0 and seq >= WINDOW + tq and tq % SLC_LEN == 0 and seq % tk == 0
    nq = seq // tq
    ncp = seq // CMP_STRIDE
    n_slc = seq // SLC_LEN
    nsp = -(-n_slc // LANES) * LANES
    n_top = min(SLC_TOPN, n_slc)
    qw = NSA_REP * LANES
    g = NSA_GROUPS
    q_cols = g * NSA_REP

    def kv_col(which):
        return pl.BlockSpec((seq, LANES), lambda b, gi, i: (b, q_cols + which * g + gi))

    head_state = pltpu.VMEM((NSA_REP, tq, LANES), F32)
    return pl.pallas_call(
        functools.partial(_nsa_kernel, tq=tq, tk=tk, ncp=ncp, nsp=nsp, n_top=n_top),
        grid=(batch, g, nq),
        in_specs=[pl.BlockSpec((tq, qw), lambda b, gi, i: (b * nq + i, gi)),
                  pl.BlockSpec((None, None, ncp, LANES), lambda b, gi, i: (b, gi, 0, 0)),
                  pl.BlockSpec((None, None, ncp, LANES), lambda b, gi, i: (b, g + gi, 0, 0)),
                  kv_col(2), kv_col(3), kv_col(4), kv_col(5),
                  pl.BlockSpec((tq, LANES), lambda b, gi, i: (b * nq + i, gi))],
        out_specs=pl.BlockSpec((tq, qw), lambda b, gi, i: (b * nq + i, gi)),
        out_shape=jax.ShapeDtypeStruct((batch * seq, g * qw), BF16),
        scratch_shapes=[head_state] * 9 + [pltpu.VMEM((tq, ncp), F32), pltpu.VMEM((tq, nsp), BF16)],
        compiler_params=_params("parallel", "parallel", "arbitrary"),
        name="nsa_attention",
    )(z, kvc, kvc, z, z, z, z, gates)


def _xattn_kernel(hb_ref, h_ref, wq_ref, kv_ref, wo_ref, g_ref, b_ref, o_ref, ob_ref, *, alpha):
    width = XATTN_HEADS * LANES
    scale = LANES ** -0.5
    q = _dot(hb_ref[...], wq_ref[...]).astype(BF16)
    kv = kv_ref[...]
    outs = []
    for hd in range(XATTN_HEADS):
        sl = slice(hd * LANES, (hd + 1) * LANES)
        s = _dot(q[:, sl], kv[:, sl], _NT) * scale
        e = jnp.exp(s - jnp.max(s, axis=-1, keepdims=True))
        p = e / jnp.sum(e, axis=-1, keepdims=True)
        outs.append(_dot(p.astype(BF16), kv[:, width + hd * LANES:width + (hd + 1) * LANES]))
    o = jnp.concatenate(outs, axis=1).astype(BF16)
    y = _dot(o, wo_ref[...])
    out = _layer_norm_rows(alpha * h_ref[...] + y, g_ref[...], b_ref[...])
    o_ref[...] = out
    ob_ref[...] = out.astype(BF16)


def xattn_block(hb, h, wq, kv, wo, g, b, alpha, batch, seq, tm=256):
    n, d = h.shape
    tm = min(tm, seq)
    nt = seq // tm
    mem_len = kv.shape[0] // batch
    width = XATTN_HEADS * LANES
    return pl.pallas_call(
        functools.partial(_xattn_kernel, alpha=alpha),
        grid=(batch, nt),
        in_specs=[pl.BlockSpec((tm, d), lambda bi, i: (bi * nt + i, 0)),
                  pl.BlockSpec((tm, d), lambda bi, i: (bi * nt + i, 0)),
                  pl.BlockSpec((d, width), lambda bi, i: (0, 0)),
                  pl.BlockSpec((mem_len, 2 * width), lambda bi, i: (bi, 0)),
                  pl.BlockSpec((width, d), lambda bi, i: (0, 0)),
                  pl.BlockSpec((1, d), lambda bi, i: (0, 0)),
                  pl.BlockSpec((1, d), lambda bi, i: (0, 0))],
        out_specs=[pl.BlockSpec((tm, d), lambda bi, i: (bi * nt + i, 0)),
                   pl.BlockSpec((tm, d), lambda bi, i: (bi * nt + i, 0))],
        out_shape=[jax.ShapeDtypeStruct((n, d), F32), jax.ShapeDtypeStruct((n, d), BF16)],
        compiler_params=_params("parallel", "parallel"),
        name="xattn_block",
    )(hb, h, wq, kv, wo, g.reshape(1, d), b.reshape(1, d))


def _peer_route_kernel(q_ref, keys_ref, mb_ref, e1_ref, b2_ref, e2_ref, *, tn):
    k = PEER_TOPK
    nk = PEER_N_KEYS
    hi = lax.Precision.HIGHEST
    for h in range(PEER_HEADS):
        s1 = _dot(keys_ref[h, 0], q_ref[:, (2 * h) * LANES:(2 * h + 1) * LANES], _NT, hi)
        s2 = _dot(keys_ref[h, 1], q_ref[:, (2 * h + 1) * LANES:(2 * h + 2) * LANES], _NT, hi)
        v1, v1_all, rank1 = _topk_rows(s1, k)
        v2, v2_all, rank2 = _topk_rows(s2, k)
        half = k // 2
        iota_a = lax.broadcasted_iota(I32, (k, tn), 0)
        is_hi = iota_a >= half
        cand = [v1_all + v2[b] for b in range(half)] + [jnp.where(is_hi, v1[0] + v2_all, -jnp.inf)]
        ids = [iota_a * k + b for b in range(half)] + [jnp.where(is_hi, iota_a, iota_a + k * k)]
        tops, _, crank = _topk_rows(jnp.concatenate(cand, axis=0), k, jnp.concatenate(ids, axis=0))
        zsum = tops[0] * 0.0
        for a in range(k):
            zsum = zsum + jnp.exp(tops[a] - tops[0])
        cnt = jnp.zeros((k, tn), F32)
        for b in range(half):
            cnt = cnt + jnp.where(crank[b * k:(b + 1) * k] < k, 1.0, 0.0)
        won_hi = (crank[half * k:(half + 1) * k] < k) & is_hi
        extra = jnp.sum(jnp.where(won_hi, 1.0, 0.0), axis=0, keepdims=True)
        cnt = (cnt + jnp.where(iota_a == 0, extra, 0.0)).astype(I32)
        mb = jnp.zeros((nk, tn), I32)
        for a in range(k):
            mb = jnp.where(rank1 == a, cnt[a:a + 1], mb)
        mb_ref[h] = mb
        b2_ref[h] = rank2.astype(b2_ref.dtype)
        e1_ref[h] = jnp.where(rank1 < k, jnp.exp(s1 - v1[0]), 0.0) / zsum
        e2_ref[h] = jnp.where(rank2 < k, jnp.exp(s2 - v2[0]), 0.0).astype(e2_ref.dtype)


def peer_route(q, keys, tn=256):
    n = q.shape[0]
    tn = min(tn, n)
    tab = lambda dt: jax.ShapeDtypeStruct((PEER_HEADS, PEER_N_KEYS, n), dt)
    tspec = pl.BlockSpec((PEER_HEADS, PEER_N_KEYS, tn), lambda i: (0, 0, i))
    return pl.pallas_call(
        functools.partial(_peer_route_kernel, tn=tn),
        grid=(n // tn,),
        in_specs=[pl.BlockSpec((tn, q.shape[1]), lambda i: (i, 0)),
                  pl.BlockSpec(keys.shape, lambda i: (0, 0, 0, 0))],
        out_specs=[tspec, tspec, tspec, tspec],
        out_shape=[tab(I32), tab(F32), tab(jnp.int16), tab(BF16)],
        compiler_params=_params("parallel"),
        name="peer_route",
    )(q, keys)


def _peer_dense_kernel(x_ref, u_ref, vt_ref, mb_ref, e1_ref, b2_ref, e2_ref, o_ref, w_ref,
                       *, te, tn):
    j = pl.program_id(1)

    @pl.when(j == 0)
    def _():
        o_ref[...] = jnp.zeros(o_ref.shape, F32)

    nk = PEER_N_KEYS
    for sub in range(te // nk):
        i1 = j * (te // nk) + sub
        w = jnp.zeros((nk, tn), BF16)
        for h in range(PEER_HEADS):
            hit = b2_ref[h] < mb_ref[h, pl.ds(i1, 1), :].astype(jnp.int16)
            gate = e1_ref[h, pl.ds(i1, 1), :].astype(BF16) * e2_ref[h]
            w = w + jnp.where(hit, gate, jnp.zeros((), BF16))
        w_ref[sub * nk:(sub + 1) * nk, :] = w.astype(F32)

    chains = list(range(0, te, PEER_CHAIN))
    hts = [_dot(u_ref[c0:c0 + PEER_CHAIN, :], x_ref[...], _NT) for c0 in chains]
    for c0, ht in zip(chains, hts):
        gelu = 0.5 * ht * (1.0 + lax.erf(ht * (2.0 ** -0.5)))
        a = (gelu * w_ref[c0:c0 + PEER_CHAIN, :]).astype(BF16)
        o_ref[...] += _dot(vt_ref[:, c0:c0 + PEER_CHAIN], a)


def peer_dense(xb, u, vt, tables, tn=512, te=512):
    n, d = xb.shape
    n_exp = u.shape[0]
    tn = min(tn, n)
    assert te % PEER_CHAIN == 0 and n_exp % te == 0
    once = pl.Buffered(1)
    tspec = pl.BlockSpec((PEER_HEADS, PEER_N_KEYS, tn), lambda i, j: (0, 0, i), pipeline_mode=once)
    return pl.pallas_call(
        functools.partial(_peer_dense_kernel, te=te, tn=tn),
        grid=(n // tn, n_exp // te),
        in_specs=[pl.BlockSpec((tn, d), lambda i, j: (i, 0), pipeline_mode=once),
                  pl.BlockSpec((te, d), lambda i, j: (j, 0)),
                  pl.BlockSpec((d, te), lambda i, j: (0, j)),
                  tspec, tspec, tspec, tspec],
        out_specs=pl.BlockSpec((d, tn), lambda i, j: (0, i)),
        out_shape=jax.ShapeDtypeStruct((d, n), F32),
        scratch_shapes=[pltpu.VMEM((te, tn), F32)],
        compiler_params=_params("parallel", "arbitrary"),
        name="peer_dense",
    )(xb, u, vt, *tables)


def odd_mixer_core(hb, w_in, cmp_pe, cmp_w, batch, seq, nsa_tq=128):
    q_w = NSA_GROUPS * NSA_REP * LANES
    kv_w = NSA_GROUPS * LANES
    nsa_main = q_w + 6 * kv_w
    n_gate = 3 * NSA_GROUPS * NSA_REP
    z = matmul(hb, w_in[:, :nsa_main].astype(BF16), BF16, scaled_cols=q_w, scale=QK_LOG2_SCALE)
    per_group = n_gate // NSA_GROUPS
    w_gate = w_in[:, nsa_main:].reshape(-1, NSA_GROUPS, per_group)
    w_gate = jnp.pad(w_gate, ((0, 0), (0, 0), (0, LANES - per_group)))
    w_gate = w_gate.reshape(-1, NSA_GROUPS * LANES).astype(BF16)
    gates = matmul(hb, w_gate, F32, act="sigmoid")
    xc = z[:, q_w:q_w + 2 * kv_w].reshape(batch, seq // CMP_STRIDE, CMP_STRIDE, 2 * NSA_GROUPS, LANES)
    xc = xc.transpose(0, 3, 1, 2, 4).reshape(
        batch, 2 * NSA_GROUPS, seq // CMP_STRIDE, CMP_STRIDE * LANES)
    half = CMP_STRIDE * LANES
    cw = cmp_w.astype(BF16)
    wcat = jnp.concatenate([cw[:, :half], cw[:, half:]], axis=2)
    pe = cmp_pe.reshape(2, 2, 1, half).astype(BF16)
    pecat = jnp.concatenate([pe[:, 0], pe[:, 1], jnp.zeros((2, 6, half), BF16)], axis=1)
    kvc = nsa_compress(xc, wcat, pecat)
    return nsa_attention(z, kvc, gates, batch, seq, tq=nsa_tq)


def even_mixer_core(hb, w_in, lam_vec, diff_g, gamma, hgrn_g, batch, seq, lam_init, layer):
    diff_w = DIFF_HEADS * 2 * LANES
    attn_cols = 3 * diff_w
    w_in = w_in.astype(BF16)
    z_attn = matmul(hb, w_in[:, :attn_cols], BF16, scaled_cols=diff_w, scale=QK_LOG2_SCALE)
    z_rec = matmul(hb, w_in[:, attn_cols:], F32)
    o_a = diff_attention(z_attn, lam_vec, diff_g, batch, seq, lam_init)
    o_b = hgrn2(z_rec, gamma, hgrn_g, batch, seq, layer)
    return jnp.concatenate([o_a, o_b], axis=1)


def peer_core(hb, wq, keys, u, v):
    pq = matmul(hb, wq.astype(BF16), F32)
    tables = peer_route(pq, keys)
    return peer_dense(hb, u.astype(BF16), v.T.astype(BF16), tables)


def kernel(x, mem, w_in_even, w_out_even, diff_lambda, diff_norm_g, hgrn_gamma, hgrn_norm_g,
           w_in_odd, w_out_odd, cmp_pe, cmp_w, xattn_wq, xattn_wkv, xattn_wo, ln_g, ln_b,
           peer_wq, peer_keys, peer_u, peer_v):
    batch, seq, d = x.shape
    depth = ln_g.shape[0]
    n = batch * seq
    assert d == 32 * LANES, "head dims are d_model/32 and must equal the lane count"
    alpha = (2.0 * depth) ** 0.25

    h = x.reshape(n, d)
    hb = h.astype(BF16)
    memb = mem.reshape(-1, d).astype(BF16)

    for l in range(depth):
        j = l // 2
        if l % 2 == 0:
            lam_init = 0.8 - 0.6 * math.exp(-0.3 * l)
            mix = even_mixer_core(hb, w_in_even[j], diff_lambda[j], diff_norm_g[j], hgrn_gamma,
                                  hgrn_norm_g[j], batch, seq, lam_init, j)
            y = matmul(mix, w_out_even[j].astype(BF16), F32)
        else:
            o = odd_mixer_core(hb, w_in_odd[j], cmp_pe[j], cmp_w[j], batch, seq)
            y = matmul(o, w_out_odd[j].astype(BF16), F32)
        h, hb = ln_residual(h, y, ln_g[l, 0], ln_b[l, 0], alpha)

        kv = matmul(memb, xattn_wkv[l].astype(BF16), BF16)
        h, hb = xattn_block(hb, h, xattn_wq[l].astype(BF16), kv, xattn_wo[l].astype(BF16),
                            ln_g[l, 1], ln_b[l, 1], alpha, batch, seq)

        yt = peer_core(hb, peer_wq[l], peer_keys[l], peer_u[l], peer_v[l])
        h, hb = ln_residual(h, yt, ln_g[l, 2], ln_b[l, 2], alpha, y_transposed=True)

    return h.reshape(batch, seq, d)
```

```python
import functools
import math

import jax
import jax.numpy as jnp
from jax import lax
from jax.experimental import pallas as pl
from jax.experimental.pallas import tpu as pltpu

F32 = jnp.float32
BF16 = jnp.bfloat16
I32 = jnp.int32

SUBLANES = 8
LANES = 128
VMEM_LIMIT_BYTES = 56 * 1024 * 1024

DIFF_HEADS = 8
HGRN_HEADS = 16
HGRN_CHUNK = 16
LB_FLOOR = 1e-30
NSA_GROUPS = 4
NSA_REP = 8
CMP_LEN = 32
CMP_STRIDE = 16
SLC_LEN = 64
SLC_TOPN = 16
WINDOW = 512
FORCE_SCORE = 1e9
XATTN_HEADS = 4
PEER_HEADS = 8
PEER_N_KEYS = 128
PEER_TOPK = 16
FLASH_LOOKAHEAD = 3
PEER_CHAIN = 256
QK_LOG2_SCALE = LANES ** -0.5 * math.log2(math.e)
LN_EPS = 1e-5
RMS_EPS = 1e-6
NEG_INF = -1e30

_NT = (((1,), (1,)), ((), ()))
_TN = (((0,), (0,)), ((), ()))


def _params(*sem):
    return pltpu.CompilerParams(dimension_semantics=sem, vmem_limit_bytes=VMEM_LIMIT_BYTES)


def _dot(a, b, dims=None, precision=None):
    if dims is None:
        return jnp.dot(a, b, preferred_element_type=F32, precision=precision)
    return lax.dot_general(a, b, dims, preferred_element_type=F32, precision=precision)


def _matmul_kernel(x_ref, w_ref, o_ref, *, act, scaled_blocks, scale):
    acc = _dot(x_ref[...], w_ref[...])
    if scaled_blocks:
        acc = acc * jnp.where(pl.program_id(1) < scaled_blocks, scale, 1.0)
    if act == "sigmoid":
        acc = jax.nn.sigmoid(acc)
    o_ref[...] = acc.astype(o_ref.dtype)


def matmul(x, w, out_dtype, act=None, scaled_cols=0, scale=1.0, tm=1024, tn=1024):
    m, k = x.shape
    n = w.shape[1]
    tm = min(tm, m)
    tn = min(tn, n)
    assert m % tm == 0 and n % tn == 0 and scaled_cols % tn == 0, (m, n, tm, tn, scaled_cols)
    return pl.pallas_call(
        functools.partial(_matmul_kernel, act=act, scaled_blocks=scaled_cols // tn, scale=scale),
        grid=(m // tm, n // tn),
        in_specs=[pl.BlockSpec((tm, k), lambda i, j: (i, 0)),
                  pl.BlockSpec((k, tn), lambda i, j: (0, j))],
        out_specs=pl.BlockSpec((tm, tn), lambda i, j: (i, j)),
        out_shape=jax.ShapeDtypeStruct((m, n), out_dtype),
        compiler_params=_params("parallel", "parallel"),
        name="matmul",
    )(x, w)


def _layer_norm_rows(z, g, b):
    mu = jnp.mean(z, axis=-1, keepdims=True)
    zc = z - mu
    var = jnp.mean(zc * zc, axis=-1, keepdims=True)
    return zc * lax.rsqrt(var + LN_EPS) * g + b


def _ln_res_kernel(h_ref, y_ref, g_ref, b_ref, o_ref, ob_ref, *, alpha, y_transposed):
    y = y_ref[...]
    if y_transposed:
        y = y.T
    out = _layer_norm_rows(alpha * h_ref[...] + y, g_ref[...], b_ref[...])
    o_ref[...] = out
    ob_ref[...] = out.astype(BF16)


def ln_residual(h, y, g, b, alpha, y_transposed=False, tm=256):
    n, d = h.shape
    tm = min(tm, n)
    y_spec = (pl.BlockSpec((d, tm), lambda i: (0, i)) if y_transposed
              else pl.BlockSpec((tm, d), lambda i: (i, 0)))
    return pl.pallas_call(
        functools.partial(_ln_res_kernel, alpha=alpha, y_transposed=y_transposed),
        grid=(n // tm,),
        in_specs=[pl.BlockSpec((tm, d), lambda i: (i, 0)), y_spec,
                  pl.BlockSpec((1, d), lambda i: (0, 0)),
                  pl.BlockSpec((1, d), lambda i: (0, 0))],
        out_specs=[pl.BlockSpec((tm, d), lambda i: (i, 0)),
                   pl.BlockSpec((tm, d), lambda i: (i, 0))],
        out_shape=[jax.ShapeDtypeStruct((n, d), F32), jax.ShapeDtypeStruct((n, d), BF16)],
        compiler_params=_params("parallel"),
        name="ln_residual",
    )(h, y, g.reshape(1, d), b.reshape(1, d))


def _lane_tile(x, width):
    reps = width // LANES
    return x if reps == 1 else jnp.concatenate([x] * reps, axis=1)


def _flash_heads(score_fns, v, m_ref, l_ref, acc_ref, psum_ref=None, row_ok=None, states=None):
    n = len(score_fns)
    dv = acc_ref.shape[-1]
    states = list(range(n)) if states is None else states
    values = v if isinstance(v, (list, tuple)) else [v] * n
    ahead = min(FLASH_LOOKAHEAD, n)
    scores = [score_fns[r]() for r in range(ahead)]
    pending = []

    def retire():
        pr, pa, ppv = pending.pop(0)
        acc_ref[pr] = _lane_tile(pa, dv) * acc_ref[pr] + ppv

    for r in range(n):
        s = scores.pop(0)
        if r + ahead < n:
            scores.append(score_fns[r + ahead]())
        st = states[r]
        m_prev = m_ref[st]
        m_next = jnp.maximum(m_prev, jnp.max(s, axis=-1, keepdims=True))
        p = jnp.exp2(s - _lane_tile(m_next, s.shape[1]))
        alpha = jnp.exp2(m_prev - m_next)
        l_new = alpha * l_ref[st] + jnp.sum(p, axis=-1, keepdims=True)
        l_ref[st] = l_new
        m_ref[st] = m_next
        if psum_ref is not None:
            psum_ref[...] += p * _lane_tile(jnp.where(row_ok, 1.0 / l_new, 0.0), s.shape[1])
        if len(pending) == ahead:
            retire()
        pending.append((st, alpha, _dot(p.astype(BF16), values[r])))
    while pending:
        retire()


def _diff_attn_kernel(lam_ref, q_ref, k_ref, v_ref, g_ref, o_ref, m_ref, l_ref, acc_ref,
                      *, tq, tk, lam_init):
    qi = pl.program_id(2)
    m_ref[...] = jnp.full(m_ref.shape, NEG_INF, F32)
    l_ref[...] = jnp.zeros(l_ref.shape, F32)
    acc_ref[...] = jnp.zeros(acc_ref.shape, F32)

    def units(c, bias):
        k0 = pl.multiple_of(c * tk, tk)

        def scores(half):
            sl = slice(half * LANES, (half + 1) * LANES)
            s = _dot(q_ref[:, sl], k_ref[pl.ds(k0, tk), sl], _NT)
            return s if bias is None else s + bias

        return [functools.partial(scores, half) for half in range(2)], v_ref[pl.ds(k0, tk), :]

    def run(chunks):
        fns, vals = [], []
        for c, bias in chunks:
            f, v = units(c, bias)
            fns += f
            vals += [v, v]
        _flash_heads(fns, vals, m_ref, l_ref, acc_ref, states=[0, 1] * len(chunks))

    def body(cp, carry):
        run([(2 * cp, None), (2 * cp + 1, None)])
        return carry

    lax.fori_loop(0, qi // 2, body, 0)
    causal = (lax.broadcasted_iota(I32, (1, tk), 1) <= lax.broadcasted_iota(I32, (tq, 1), 0))
    diag_bias = jnp.where(causal, 0.0, NEG_INF)

    @pl.when(qi % 2 == 1)
    def _():
        run([(qi - 1, None), (qi, diag_bias)])

    @pl.when(qi % 2 == 0)
    def _():
        run([(qi, diag_bias)])


    lv = lam_ref[...]
    lam = (jnp.exp(jnp.sum(lv[0:1] * lv[1:2], axis=-1, keepdims=True))
           - jnp.exp(jnp.sum(lv[2:3] * lv[3:4], axis=-1, keepdims=True)) + lam_init)
    dv = acc_ref.shape[-1]
    o = (acc_ref[0] / _lane_tile(l_ref[0], dv) - lam * (acc_ref[1] / _lane_tile(l_ref[1], dv)))
    o = o * lax.rsqrt(jnp.mean(o * o, axis=-1, keepdims=True) + RMS_EPS)
    o_ref[...] = (o * g_ref[...] * (1.0 - lam_init)).astype(o_ref.dtype)


def diff_attention(z, lam_vec, norm_g, batch, seq, lam_init, tq=512):
    dv = 2 * LANES
    tq = tk = min(tq, seq)
    nq = seq // tq
    return pl.pallas_call(
        functools.partial(_diff_attn_kernel, tq=tq, tk=tk, lam_init=lam_init),
        grid=(batch, DIFF_HEADS, nq),
        in_specs=[pl.BlockSpec((4, LANES), lambda b, h, i: (0, 0)),
                  pl.BlockSpec((tq, dv), lambda b, h, i: (b * nq + i, h)),
                  pl.BlockSpec((seq, dv), lambda b, h, i: (b, DIFF_HEADS + h)),
                  pl.BlockSpec((seq, dv), lambda b, h, i: (b, 2 * DIFF_HEADS + h)),
                  pl.BlockSpec((1, dv), lambda b, h, i: (0, 0))],
        out_specs=pl.BlockSpec((tq, dv), lambda b, h, i: (b * nq + i, h)),
        out_shape=jax.ShapeDtypeStruct((batch * seq, DIFF_HEADS * dv), BF16),
        scratch_shapes=[pltpu.VMEM((2, tq, LANES), F32), pltpu.VMEM((2, tq, LANES), F32),
                        pltpu.VMEM((2, tq, dv), F32)],
        compiler_params=_params("parallel", "parallel", "arbitrary"),
        name="diff_attention",
    )(lam_vec, z, z, z, norm_g.reshape(1, dv))


def _hgrn_kernel(gam_ref, q_ref, f_ref, i_ref, gate_ref, ng_ref, o_ref,
                 st_ref, *, tt, layer, heads):
    c_len = HGRN_CHUNK
    width = heads * LANES

    @pl.when(pl.program_id(2) == 0)
    def _():
        st_ref[...] = jnp.zeros(st_ref.shape, F32)

    gam = gam_ref[...]
    e = jnp.exp(gam - jnp.max(gam, axis=0, keepdims=True))
    p = e / jnp.sum(e, axis=0, keepdims=True)
    lb = jnp.zeros((1, width), F32)
    for r in range(1, layer + 1):
        lb = lb + p[r:r + 1]
    log_lb = jnp.log(jnp.maximum(lb, LB_FLOOR))
    log_1m = jnp.log1p(-lb)
    trow = lax.broadcasted_iota(I32, (c_len, 1), 0)
    ng = ng_ref[...]

    def body(c, carry):
        rows = pl.ds(pl.multiple_of(c * c_len, c_len), c_len)
        z = f_ref[rows, :]
        log_sig = jnp.minimum(z, 0.0) - jnp.log1p(jnp.exp(-jnp.abs(z)))
        log_gate = log_1m + log_sig
        b_all = (jnp.maximum(log_lb, log_gate)
                 + jnp.log1p(jnp.exp(-jnp.abs(log_lb - log_gate))))
        step = 1
        while step < c_len:
            b_all = b_all + jnp.where(trow >= step, pltpu.roll(b_all, step, axis=0), 0.0)
            step *= 2
        k_all = (1.0 - lb) * jax.nn.sigmoid(-z)
        q_all = jax.nn.silu(q_ref[rows, :])
        g_all = jax.nn.silu(gate_ref[rows, :])
        for hd in range(heads):
            cols = slice(hd * LANES, (hd + 1) * LANES)
            bc, qc, kc = b_all[:, cols], q_all[:, cols], k_all[:, cols]
            ic = i_ref[rows, cols]
            st = st_ref[hd]
            o = _dot((qc * jnp.exp(bc)).astype(BF16), st.astype(BF16), _NT)
            tiles = [slice(r0, r0 + SUBLANES) for r0 in range(0, c_len, SUBLANES)]
            outs = [o[t] for t in tiles]
            for s in range(c_len):
                for n, t in enumerate(tiles):
                    if t.stop <= s:
                        continue
                    es = jnp.exp(bc[t] - bc[s:s + 1])
                    if t.start <= s:
                        es = jnp.where(trow[t] >= s, es, 0.0)
                    col = jnp.sum(qc[t] * es * kc[s:s + 1], axis=-1, keepdims=True)
                    outs[n] = outs[n] + col * ic[s:s + 1]
            o = jnp.concatenate(outs, axis=0)
            b_last = bc[c_len - 1:c_len]
            kd = kc * jnp.exp(b_last - bc)
            st_ref[hd] = st * jnp.exp(b_last) + _dot(ic.astype(BF16), kd.astype(BF16), _TN)
            y = o * lax.rsqrt(jnp.mean(o * o, axis=-1, keepdims=True) + RMS_EPS) * ng
            o_ref[rows, cols] = (y * g_all[:, cols]).astype(o_ref.dtype)
        return carry

    lax.fori_loop(0, tt // c_len, body, 0)


def hgrn2(z, gamma, norm_g, batch, seq, layer, tt=512, heads=4):
    tt = min(tt, seq)
    nt = seq // tt
    n_even = gamma.shape[0]
    ng = HGRN_HEADS // heads
    width = heads * LANES

    def col(offset):
        return pl.BlockSpec((tt, width), lambda b, h, t: (b * nt + t, offset * ng + h))

    return pl.pallas_call(
        functools.partial(_hgrn_kernel, tt=tt, layer=layer, heads=heads),
        grid=(batch, ng, nt),
        in_specs=[pl.BlockSpec((n_even, width), lambda b, h, t: (0, h)),
                  col(0), col(1), col(2), col(3),
                  pl.BlockSpec((1, LANES), lambda b, h, t: (0, 0))],
        out_specs=pl.BlockSpec((tt, width), lambda b, h, t: (b * nt + t, h)),
        out_shape=jax.ShapeDtypeStruct((batch * seq, HGRN_HEADS * LANES), BF16),
        scratch_shapes=[pltpu.VMEM((heads, LANES, LANES), F32)],
        compiler_params=_params("parallel", "parallel", "arbitrary"),
        name="hgrn2",
    )(gamma, z, z, z, z, norm_g.reshape(1, LANES))


def _topk_rows(vals, k, ids=None):
    r, n = vals.shape
    iota0 = lax.broadcasted_iota(I32, (r, n), 0) if ids is None else ids
    iota_k = lax.broadcasted_iota(I32, (k, n), 0)
    rank = jnp.full((r, n), k, I32)
    tops = []
    stacked = jnp.zeros((k, n), F32)
    for a in range(k):
        m = jnp.max(vals, axis=0, keepdims=True)
        idx = jnp.min(jnp.where(vals == m, iota0, jnp.iinfo(jnp.int32).max), axis=0, keepdims=True)
        hit = iota0 == idx
        rank = jnp.where(hit, a, rank)
        tops.append(m)
        stacked = jnp.where(iota_k == a, m, stacked)
        vals = jnp.where(hit, -jnp.inf, vals)
    return tops, stacked, rank


def _compress_kernel(x_ref, w_ref, pe_ref, o_ref):
    pq = _dot(x_ref[...], w_ref[...])
    nrow = pq.shape[0]
    pe_c = _dot(pe_ref[...], w_ref[...])
    const = pe_c[0:1, :LANES] + pe_c[1:2, LANES:]
    nxt = pltpu.roll(pq[:, LANES:], nrow - 1, axis=0)
    o_ref[...] = (pq[:, :LANES] + nxt + const).astype(o_ref.dtype)


def nsa_compress(xc, wcat, pecat):
    bsz, nj, nrow, kk = xc.shape
    return pl.pallas_call(
        _compress_kernel,
        grid=(bsz, nj),
        in_specs=[pl.BlockSpec((None, None, nrow, kk), lambda b, j: (b, j, 0, 0)),
                  pl.BlockSpec((None, kk, 2 * LANES), lambda b, j: (j // NSA_GROUPS, 0, 0)),
                  pl.BlockSpec((None, 8, kk), lambda b, j: (j // NSA_GROUPS, 0, 0))],
        out_specs=pl.BlockSpec((None, None, nrow, LANES), lambda b, j: (b, j, 0, 0)),
        out_shape=jax.ShapeDtypeStruct((bsz, nj, nrow, LANES), BF16),
        compiler_params=_params("parallel", "parallel"),
        name="nsa_compress",
    )(xc, wcat, pecat)


def _nsa_kernel(q_ref, kc_ref, vc_ref, ks_ref, vs_ref, kw_ref, vw_ref, gt_ref, o_ref,
                m_ref, l_ref, acc_ref, mw_ref, lw_ref, accw_ref, mc_ref, lc_ref, accc_ref,
                psum_ref, sel_ref, *, tq, tk, ncp, nsp, n_top):
    rep = NSA_REP
    q0 = pl.program_id(2) * tq
    qpos = q0 + lax.broadcasted_iota(I32, (tq, 1), 0)

    def q_head(r):
        return q_ref[:, r * LANES:(r + 1) * LANES]

    def once(fn):
        lax.fori_loop(0, jnp.minimum(pl.program_id(2), 0) + 1, lambda _, carry: (fn(), carry)[1], 0)

    has_key = qpos >= CMP_LEN - 1

    def compressed():
        for ref in (m_ref, mw_ref, mc_ref):
            ref[...] = jnp.full(ref.shape, NEG_INF, F32)
        for ref in (l_ref, acc_ref, lw_ref, accw_ref, lc_ref, accc_ref):
            ref[...] = jnp.zeros(ref.shape, F32)
        cend = lax.broadcasted_iota(I32, (1, ncp), 1) * CMP_STRIDE + (CMP_LEN - 1)
        bias_c = jnp.where(cend <= qpos, 0.0, NEG_INF)
        psum_ref[...] = jnp.zeros(psum_ref.shape, F32)
        kc = kc_ref[...]
        _flash_heads([functools.partial(lambda r: _dot(q_head(r), kc, _NT) + bias_c, r)
                      for r in range(rep)], vc_ref[...], mc_ref, lc_ref, accc_ref, psum_ref, has_key)

    once(compressed)

    span = WINDOW + tq
    start = pl.multiple_of(jnp.maximum(q0 - WINDOW, 0), tq)

    def window_and_selection():
        n_i = lax.broadcasted_iota(I32, (ncp, nsp), 0)
        j4 = lax.broadcasted_iota(I32, (ncp, nsp), 1) * (SLC_LEN // CMP_STRIDE)
        overlap = ((n_i >= j4 - (CMP_LEN - 1) // CMP_STRIDE)
                   & (n_i < j4 + SLC_LEN // CMP_STRIDE)).astype(F32)
        imp_t = _dot(psum_ref[...], overlap, precision=lax.Precision.HIGHEST).T
        jt = lax.broadcasted_iota(I32, (nsp, 1), 0)
        cur = (q0 + lax.broadcasted_iota(I32, (1, tq), 1)) // SLC_LEN
        forced = (jt == 0) | (jt == cur) | (jt == cur - 1)
        imp_t = jnp.where(forced, FORCE_SCORE, jnp.where(jt <= cur, imp_t, -1.0))
        _, _, rank = _topk_rows(imp_t, n_top)
        sel_ref[...] = (rank < n_top).astype(F32).T.astype(BF16)

        kpos = start + lax.broadcasted_iota(I32, (1, span), 1)
        bias_w = jnp.where((kpos <= qpos) & (kpos > qpos - WINDOW), 0.0, NEG_INF)
        k_w = kw_ref[pl.ds(start, span), :]
        _flash_heads([functools.partial(lambda r: _dot(q_head(r), k_w, _NT) + bias_w, r)
                      for r in range(rep)], vw_ref[pl.ds(start, span), :], mw_ref, lw_ref, accw_ref)

    once(window_and_selection)

    jj = lax.broadcasted_iota(I32, (nsp, 1), 0)

    def body(c, carry):
        k0 = pl.multiple_of(c * tk, tk)
        k = ks_ref[pl.ds(k0, tk), :]
        v = vs_ref[pl.ds(k0, tk), :]
        kpos = k0 + lax.broadcasted_iota(I32, (1, tk), 1)
        expand = (jj == kpos // SLC_LEN).astype(BF16)
        allowed = (_dot(sel_ref[...], expand) > 0.5) & (kpos <= qpos)
        bias = jnp.where(allowed, 0.0, NEG_INF)
        _flash_heads([functools.partial(lambda r: _dot(q_head(r), k, _NT) + bias, r)
                      for r in range(rep)], v, m_ref, l_ref, acc_ref)
        return carry

    lax.fori_loop(0, (q0 + tq + tk - 1) // tk, body, 0)

    gt = gt_ref[...]
    for r in range(rep):
        o = (gt[:, 3 * r:3 * r + 1] * jnp.where(has_key, accc_ref[r] / lc_ref[r], 0.0)
             + gt[:, 3 * r + 1:3 * r + 2] * (acc_ref[r] / l_ref[r])
             + gt[:, 3 * r + 2:3 * r + 3] * (accw_ref[r] / lw_ref[r]))
        o_ref[:, r * LANES:(r + 1) * LANES] = o.astype(o_ref.dtype)


def nsa_attention(z, kvc, gates, batch, seq, tq=128, tk=1024):
    tk = min(tk, seq)
    assert seq % tq == 0 and seq >= WINDOW + tq and tq % SLC_LEN == 0 and seq % tk == 0
    nq = seq // tq
    ncp = seq // CMP_STRIDE
    n_slc = seq // SLC_LEN
    nsp = -(-n_slc // LANES) * LANES
    n_top = min(SLC_TOPN, n_slc)
    qw = NSA_REP * LANES
    g = NSA_GROUPS
    q_cols = g * NSA_REP

    def kv_col(which):
        return pl.BlockSpec((seq, LANES), lambda b, gi, i: (b, q_cols + which * g + gi))

    head_state = pltpu.VMEM((NSA_REP, tq, LANES), F32)
    return pl.pallas_call(
        functools.partial(_nsa_kernel, tq=tq, tk=tk, ncp=ncp, nsp=nsp, n_top=n_top),
        grid=(batch, g, nq),
        in_specs=[pl.BlockSpec((tq, qw), lambda b, gi, i: (b * nq + i, gi)),
                  pl.BlockSpec((None, None, ncp, LANES), lambda b, gi, i: (b, gi, 0, 0)),
                  pl.BlockSpec((None, None, ncp, LANES), lambda b, gi, i: (b, g + gi, 0, 0)),
                  kv_col(2), kv_col(3), kv_col(4), kv_col(5),
                  pl.BlockSpec((tq, LANES), lambda b, gi, i: (b * nq + i, gi))],
        out_specs=pl.BlockSpec((tq, qw), lambda b, gi, i: (b * nq + i, gi)),
        out_shape=jax.ShapeDtypeStruct((batch * seq, g * qw), BF16),
        scratch_shapes=[head_state] * 9 + [pltpu.VMEM((tq, ncp), F32), pltpu.VMEM((tq, nsp), BF16)],
        compiler_params=_params("parallel", "parallel", "arbitrary"),
        name="nsa_attention",
    )(z, kvc, kvc, z, z, z, z, gates)


def _xattn_kernel(hb_ref, h_ref, wq_ref, kv_ref, wo_ref, g_ref, b_ref, o_ref, ob_ref, *, alpha):
    width = XATTN_HEADS * LANES
    scale = LANES ** -0.5
    q = _dot(hb_ref[...], wq_ref[...]).astype(BF16)
    kv = kv_ref[...]
    outs = []
    for hd in range(XATTN_HEADS):
        sl = slice(hd * LANES, (hd + 1) * LANES)
        s = _dot(q[:, sl], kv[:, sl], _NT) * scale
        e = jnp.exp(s - jnp.max(s, axis=-1, keepdims=True))
        p = e / jnp.sum(e, axis=-1, keepdims=True)
        outs.append(_dot(p.astype(BF16), kv[:, width + hd * LANES:width + (hd + 1) * LANES]))
    o = jnp.concatenate(outs, axis=1).astype(BF16)
    y = _dot(o, wo_ref[...])
    out = _layer_norm_rows(alpha * h_ref[...] + y, g_ref[...], b_ref[...])
    o_ref[...] = out
    ob_ref[...] = out.astype(BF16)


def xattn_block(hb, h, wq, kv, wo, g, b, alpha, batch, seq, tm=256):
    n, d = h.shape
    tm = min(tm, seq)
    nt = seq // tm
    mem_len = kv.shape[0] // batch
    width = XATTN_HEADS * LANES
    return pl.pallas_call(
        functools.partial(_xattn_kernel, alpha=alpha),
        grid=(batch, nt),
        in_specs=[pl.BlockSpec((tm, d), lambda bi, i: (bi * nt + i, 0)),
                  pl.BlockSpec((tm, d), lambda bi, i: (bi * nt + i, 0)),
                  pl.BlockSpec((d, width), lambda bi, i: (0, 0)),
                  pl.BlockSpec((mem_len, 2 * width), lambda bi, i: (bi, 0)),
                  pl.BlockSpec((width, d), lambda bi, i: (0, 0)),
                  pl.BlockSpec((1, d), lambda bi, i: (0, 0)),
                  pl.BlockSpec((1, d), lambda bi, i: (0, 0))],
        out_specs=[pl.BlockSpec((tm, d), lambda bi, i: (bi * nt + i, 0)),
                   pl.BlockSpec((tm, d), lambda bi, i: (bi * nt + i, 0))],
        out_shape=[jax.ShapeDtypeStruct((n, d), F32), jax.ShapeDtypeStruct((n, d), BF16)],
        compiler_params=_params("parallel", "parallel"),
        name="xattn_block",
    )(hb, h, wq, kv, wo, g.reshape(1, d), b.reshape(1, d))


def _peer_route_kernel(q_ref, keys_ref, mb_ref, e1_ref, b2_ref, e2_ref, *, tn):
    k = PEER_TOPK
    nk = PEER_N_KEYS
    hi = lax.Precision.HIGHEST
    for h in range(PEER_HEADS):
        s1 = _dot(keys_ref[h, 0], q_ref[:, (2 * h) * LANES:(2 * h + 1) * LANES], _NT, hi)
        s2 = _dot(keys_ref[h, 1], q_ref[:, (2 * h + 1) * LANES:(2 * h + 2) * LANES], _NT, hi)
        v1, v1_all, rank1 = _topk_rows(s1, k)
        v2, v2_all, rank2 = _topk_rows(s2, k)
        half = k // 2
        iota_a = lax.broadcasted_iota(I32, (k, tn), 0)
        is_hi = iota_a >= half
        cand = [v1_all + v2[b] for b in range(half)] + [jnp.where(is_hi, v1[0] + v2_all, -jnp.inf)]
        ids = [iota_a * k + b for b in range(half)] + [jnp.where(is_hi, iota_a, iota_a + k * k)]
        tops, _, crank = _topk_rows(jnp.concatenate(cand, axis=0), k, jnp.concatenate(ids, axis=0))
        zsum = tops[0] * 0.0
        for a in range(k):
            zsum = zsum + jnp.exp(tops[a] - tops[0])
        cnt = jnp.zeros((k, tn), F32)
        for b in range(half):
            cnt = cnt + jnp.where(crank[b * k:(b + 1) * k] < k, 1.0, 0.0)
        won_hi = (crank[half * k:(half + 1) * k] < k) & is_hi
        extra = jnp.sum(jnp.where(won_hi, 1.0, 0.0), axis=0, keepdims=True)
        cnt = (cnt + jnp.where(iota_a == 0, extra, 0.0)).astype(I32)
        mb = jnp.zeros((nk, tn), I32)
        for a in range(k):
            mb = jnp.where(rank1 == a, cnt[a:a + 1], mb)
        mb_ref[h] = mb
        b2_ref[h] = rank2.astype(b2_ref.dtype)
        e1_ref[h] = jnp.where(rank1 < k, jnp.exp(s1 - v1[0]), 0.0) / zsum
        e2_ref[h] = jnp.where(rank2 < k, jnp.exp(s2 - v2[0]), 0.0).astype(e2_ref.dtype)


def peer_route(q, keys, tn=256):
    n = q.shape[0]
    tn = min(tn, n)
    tab = lambda dt: jax.ShapeDtypeStruct((PEER_HEADS, PEER_N_KEYS, n), dt)
    tspec = pl.BlockSpec((PEER_HEADS, PEER_N_KEYS, tn), lambda i: (0, 0, i))
    return pl.pallas_call(
        functools.partial(_peer_route_kernel, tn=tn),
        grid=(n // tn,),
        in_specs=[pl.BlockSpec((tn, q.shape[1]), lambda i: (i, 0)),
                  pl.BlockSpec(keys.shape, lambda i: (0, 0, 0, 0))],
        out_specs=[tspec, tspec, tspec, tspec],
        out_shape=[tab(I32), tab(F32), tab(jnp.int16), tab(BF16)],
        compiler_params=_params("parallel"),
        name="peer_route",
    )(q, keys)


def _peer_dense_kernel(x_ref, u_ref, vt_ref, mb_ref, e1_ref, b2_ref, e2_ref, o_ref, w_ref,
                       *, te, tn):
    j = pl.program_id(1)

    @pl.when(j == 0)
    def _():
        o_ref[...] = jnp.zeros(o_ref.shape, F32)

    nk = PEER_N_KEYS
    for sub in range(te // nk):
        i1 = j * (te // nk) + sub
        w = jnp.zeros((nk, tn), BF16)
        for h in range(PEER_HEADS):
            hit = b2_ref[h] < mb_ref[h, pl.ds(i1, 1), :].astype(jnp.int16)
            gate = e1_ref[h, pl.ds(i1, 1), :].astype(BF16) * e2_ref[h]
            w = w + jnp.where(hit, gate, jnp.zeros((), BF16))
        w_ref[sub * nk:(sub + 1) * nk, :] = w

    chains = list(range(0, te, PEER_CHAIN))
    hts = [_dot(u_ref[c0:c0 + PEER_CHAIN, :], x_ref[...], _NT) for c0 in chains]
    for c0, ht in zip(chains, hts):
        gelu = 0.5 * ht * (1.0 + lax.erf(ht * (2.0 ** -0.5)))
        a = gelu.astype(BF16) * w_ref[c0:c0 + PEER_CHAIN, :]
        o_ref[...] += _dot(vt_ref[:, c0:c0 + PEER_CHAIN], a)


def peer_dense(xb, u, vt, tables, tn=512, te=512):
    n, d = xb.shape
    n_exp = u.shape[0]
    tn = min(tn, n)
    assert te % PEER_CHAIN == 0 and n_exp % te == 0
    once = pl.Buffered(1)
    tspec = pl.BlockSpec((PEER_HEADS, PEER_N_KEYS, tn), lambda i, j: (0, 0, i), pipeline_mode=once)
    return pl.pallas_call(
        functools.partial(_peer_dense_kernel, te=te, tn=tn),
        grid=(n // tn, n_exp // te),
        in_specs=[pl.BlockSpec((tn, d), lambda i, j: (i, 0), pipeline_mode=once),
                  pl.BlockSpec((te, d), lambda i, j: (j, 0)),
                  pl.BlockSpec((d, te), lambda i, j: (0, j)),
                  tspec, tspec, tspec, tspec],
        out_specs=pl.BlockSpec((d, tn), lambda i, j: (0, i)),
        out_shape=jax.ShapeDtypeStruct((d, n), F32),
        scratch_shapes=[pltpu.VMEM((te, tn), BF16)],
        compiler_params=_params("parallel", "arbitrary"),
        name="peer_dense",
    )(xb, u, vt, *tables)


def odd_mixer_core(hb, w_in, cmp_pe, cmp_w, batch, seq, nsa_tq=128):
    q_w = NSA_GROUPS * NSA_REP * LANES
    kv_w = NSA_GROUPS * LANES
    nsa_main = q_w + 6 * kv_w
    n_gate = 3 * NSA_GROUPS * NSA_REP
    z = matmul(hb, w_in[:, :nsa_main].astype(BF16), BF16, scaled_cols=q_w, scale=QK_LOG2_SCALE)
    per_group = n_gate // NSA_GROUPS
    w_gate = w_in[:, nsa_main:].reshape(-1, NSA_GROUPS, per_group)
    w_gate = jnp.pad(w_gate, ((0, 0), (0, 0), (0, LANES - per_group)))
    w_gate = w_gate.reshape(-1, NSA_GROUPS * LANES).astype(BF16)
    gates = matmul(hb, w_gate, F32, act="sigmoid")
    xc = z[:, q_w:q_w + 2 * kv_w].reshape(batch, seq // CMP_STRIDE, CMP_STRIDE, 2 * NSA_GROUPS, LANES)
    xc = xc.transpose(0, 3, 1, 2, 4).reshape(
        batch, 2 * NSA_GROUPS, seq // CMP_STRIDE, CMP_STRIDE * LANES)
    half = CMP_STRIDE * LANES
    cw = cmp_w.astype(BF16)
    wcat = jnp.concatenate([cw[:, :half], cw[:, half:]], axis=2)
    pe = cmp_pe.reshape(2, 2, 1, half).astype(BF16)
    pecat = jnp.concatenate([pe[:, 0], pe[:, 1], jnp.zeros((2, 6, half), BF16)], axis=1)
    kvc = nsa_compress(xc, wcat, pecat)
    return nsa_attention(z, kvc, gates, batch, seq, tq=nsa_tq)


def even_mixer_core(hb, w_in, lam_vec, diff_g, gamma, hgrn_g, batch, seq, lam_init, layer):
    diff_w = DIFF_HEADS * 2 * LANES
    attn_cols = 3 * diff_w
    w_in = w_in.astype(BF16)
    z_attn = matmul(hb, w_in[:, :attn_cols], BF16, scaled_cols=diff_w, scale=QK_LOG2_SCALE)
    z_rec = matmul(hb, w_in[:, attn_cols:], F32)
    o_a = diff_attention(z_attn, lam_vec, diff_g, batch, seq, lam_init)
    o_b = hgrn2(z_rec, gamma, hgrn_g, batch, seq, layer)
    return jnp.concatenate([o_a, o_b], axis=1)


def peer_core(hb, wq, keys, u, v):
    pq = matmul(hb, wq.astype(BF16), F32)
    tables = peer_route(pq, keys)
    return peer_dense(hb, u.astype(BF16), v.T.astype(BF16), tables)


def kernel(x, mem, w_in_even, w_out_even, diff_lambda, diff_norm_g, hgrn_gamma, hgrn_norm_g,
           w_in_odd, w_out_odd, cmp_pe, cmp_w, xattn_wq, xattn_wkv, xattn_wo, ln_g, ln_b,
           peer_wq, peer_keys, peer_u, peer_v):
    batch, seq, d = x.shape
    depth = ln_g.shape[0]
    n = batch * seq
    assert d == 32 * LANES, "head dims are d_model/32 and must equal the lane count"
    alpha = (2.0 * depth) ** 0.25

    h = x.reshape(n, d)
    hb = h.astype(BF16)
    memb = mem.reshape(-1, d).astype(BF16)

    for l in range(depth):
        j = l // 2
        if l % 2 == 0:
            lam_init = 0.8 - 0.6 * math.exp(-0.3 * l)
            mix = even_mixer_core(hb, w_in_even[j], diff_lambda[j], diff_norm_g[j], hgrn_gamma,
                                  hgrn_norm_g[j], batch, seq, lam_init, j)
            y = matmul(mix, w_out_even[j].astype(BF16), F32)
        else:
            o = odd_mixer_core(hb, w_in_odd[j], cmp_pe[j], cmp_w[j], batch, seq)
            y = matmul(o, w_out_odd[j].astype(BF16), F32)
        h, hb = ln_residual(h, y, ln_g[l, 0], ln_b[l, 0], alpha)

        kv = matmul(memb, xattn_wkv[l].astype(BF16), BF16)
        h, hb = xattn_block(hb, h, xattn_wq[l].astype(BF16), kv, xattn_wo[l].astype(BF16),
                            ln_g[l, 1], ln_b[l, 1], alpha, batch, seq)

        yt = peer_core(hb, peer_wq[l], peer_keys[l], peer_u[l], peer_v[l])
        h, hb = ln_residual(h, yt, ln_g[l, 2], ln_b[l, 2], alpha, y_transposed=True)

    return h.reshape(batch, seq, d)
```

```python
import functools
import math

import jax
import jax.numpy as jnp
from jax import lax
from jax.experimental import pallas as pl
from jax.experimental.pallas import tpu as pltpu

F32 = jnp.float32
BF16 = jnp.bfloat16
I32 = jnp.int32

SUBLANES = 8
LANES = 128
VMEM_LIMIT_BYTES = 56 * 1024 * 1024

DIFF_HEADS = 8
HGRN_HEADS = 16
HGRN_CHUNK = 16
LB_FLOOR = 1e-30
NSA_GROUPS = 4
NSA_REP = 8
CMP_LEN = 32
CMP_STRIDE = 16
SLC_LEN = 64
SLC_TOPN = 16
WINDOW = 512
FORCE_SCORE = 1e9
XATTN_HEADS = 4
PEER_HEADS = 8
PEER_N_KEYS = 128
PEER_TOPK = 16
FLASH_LOOKAHEAD = 3
PEER_CHAIN = 256
QK_LOG2_SCALE = LANES ** -0.5 * math.log2(math.e)
LN_EPS = 1e-5
RMS_EPS = 1e-6
NEG_INF = -1e30

_NT = (((1,), (1,)), ((), ()))
_TN = (((0,), (0,)), ((), ()))


def _params(*sem):
    return pltpu.CompilerParams(dimension_semantics=sem, vmem_limit_bytes=VMEM_LIMIT_BYTES)


def _dot(a, b, dims=None, precision=None):
    if dims is None:
        return jnp.dot(a, b, preferred_element_type=F32, precision=precision)
    return lax.dot_general(a, b, dims, preferred_element_type=F32, precision=precision)


def _matmul_kernel(x_ref, w_ref, o_ref, *, act, scaled_blocks, scale):
    acc = _dot(x_ref[...], w_ref[...])
    if scaled_blocks:
        acc = acc * jnp.where(pl.program_id(1) < scaled_blocks, scale, 1.0)
    if act == "sigmoid":
        acc = jax.nn.sigmoid(acc)
    o_ref[...] = acc.astype(o_ref.dtype)


def matmul(x, w, out_dtype, act=None, scaled_cols=0, scale=1.0, tm=1024, tn=1024):
    m, k = x.shape
    n = w.shape[1]
    tm = min(tm, m)
    tn = min(tn, n)
    assert m % tm == 0 and n % tn == 0 and scaled_cols % tn == 0, (m, n, tm, tn, scaled_cols)
    return pl.pallas_call(
        functools.partial(_matmul_kernel, act=act, scaled_blocks=scaled_cols // tn, scale=scale),
        grid=(m // tm, n // tn),
        in_specs=[pl.BlockSpec((tm, k), lambda i, j: (i, 0)),
                  pl.BlockSpec((k, tn), lambda i, j: (0, j))],
        out_specs=pl.BlockSpec((tm, tn), lambda i, j: (i, j)),
        out_shape=jax.ShapeDtypeStruct((m, n), out_dtype),
        compiler_params=_params("parallel", "parallel"),
        name="matmul",
    )(x, w)


def _matmul_pair_kernel(xa_ref, xb_ref, wa_ref, wb_ref, o_ref):
    o_ref[...] = _dot(xa_ref[...], wa_ref[...]) + _dot(xb_ref[...], wb_ref[...])


def matmul_pair(xa, xb, w, tm=1024, tn=1024):
    m, kh = xa.shape
    n = w.shape[1]
    tm = min(tm, m)
    tn = min(tn, n)
    assert xb.shape == xa.shape and w.shape[0] == 2 * kh and m % tm == 0 and n % tn == 0
    return pl.pallas_call(
        _matmul_pair_kernel,
        grid=(m // tm, n // tn),
        in_specs=[pl.BlockSpec((tm, kh), lambda i, j: (i, 0)),
                  pl.BlockSpec((tm, kh), lambda i, j: (i, 0)),
                  pl.BlockSpec((kh, tn), lambda i, j: (0, j)),
                  pl.BlockSpec((kh, tn), lambda i, j: (1, j))],
        out_specs=pl.BlockSpec((tm, tn), lambda i, j: (i, j)),
        out_shape=jax.ShapeDtypeStruct((m, n), F32),
        compiler_params=_params("parallel", "parallel"),
        name="matmul_pair",
    )(xa, xb, w, w)


def _layer_norm_rows(z, g, b):
    mu = jnp.mean(z, axis=-1, keepdims=True)
    zc = z - mu
    var = jnp.mean(zc * zc, axis=-1, keepdims=True)
    return zc * lax.rsqrt(var + LN_EPS) * g + b


def _ln_res_kernel(h_ref, y_ref, g_ref, b_ref, o_ref, ob_ref, *, alpha, y_transposed):
    y = y_ref[...]
    if y_transposed:
        y = y.T
    out = _layer_norm_rows(alpha * h_ref[...] + y, g_ref[...], b_ref[...])
    o_ref[...] = out
    ob_ref[...] = out.astype(BF16)


def ln_residual(h, y, g, b, alpha, y_transposed=False, tm=256):
    n, d = h.shape
    tm = min(tm, n)
    y_spec = (pl.BlockSpec((d, tm), lambda i: (0, i)) if y_transposed
              else pl.BlockSpec((tm, d), lambda i: (i, 0)))
    return pl.pallas_call(
        functools.partial(_ln_res_kernel, alpha=alpha, y_transposed=y_transposed),
        grid=(n // tm,),
        in_specs=[pl.BlockSpec((tm, d), lambda i: (i, 0)), y_spec,
                  pl.BlockSpec((1, d), lambda i: (0, 0)),
                  pl.BlockSpec((1, d), lambda i: (0, 0))],
        out_specs=[pl.BlockSpec((tm, d), lambda i: (i, 0)),
                   pl.BlockSpec((tm, d), lambda i: (i, 0))],
        out_shape=[jax.ShapeDtypeStruct((n, d), F32), jax.ShapeDtypeStruct((n, d), BF16)],
        compiler_params=_params("parallel"),
        name="ln_residual",
    )(h, y, g.reshape(1, d), b.reshape(1, d))


def _lane_tile(x, width):
    reps = width // LANES
    return x if reps == 1 else jnp.concatenate([x] * reps, axis=1)


def _flash_heads(score_fns, v, m_ref, l_ref, acc_ref, psum_ref=None, row_ok=None, states=None):
    n = len(score_fns)
    dv = acc_ref.shape[-1]
    states = list(range(n)) if states is None else states
    values = v if isinstance(v, (list, tuple)) else [v] * n
    ahead = min(FLASH_LOOKAHEAD, n)
    scores = [score_fns[r]() for r in range(ahead)]
    pending = []

    def retire():
        pr, pa, ppv = pending.pop(0)
        acc_ref[pr] = _lane_tile(pa, dv) * acc_ref[pr] + ppv

    for r in range(n):
        s = scores.pop(0)
        if r + ahead < n:
            scores.append(score_fns[r + ahead]())
        st = states[r]
        m_prev = m_ref[st]
        m_next = jnp.maximum(m_prev, jnp.max(s, axis=-1, keepdims=True))
        p = jnp.exp2(s - _lane_tile(m_next, s.shape[1]))
        alpha = jnp.exp2(m_prev - m_next)
        l_new = alpha * l_ref[st] + jnp.sum(p, axis=-1, keepdims=True)
        l_ref[st] = l_new
        m_ref[st] = m_next
        if psum_ref is not None:
            psum_ref[...] += p * _lane_tile(jnp.where(row_ok, 1.0 / l_new, 0.0), s.shape[1])
        if len(pending) == ahead:
            retire()
        pending.append((st, alpha, _dot(p.astype(BF16), values[r])))
    while pending:
        retire()


def _diff_attn_kernel(lam_ref, q_ref, k_ref, v_ref, g_ref, o_ref, m_ref, l_ref, acc_ref,
                      *, tq, tk, lam_init):
    qi = pl.program_id(2)
    m_ref[...] = jnp.full(m_ref.shape, NEG_INF, F32)
    l_ref[...] = jnp.zeros(l_ref.shape, F32)
    acc_ref[...] = jnp.zeros(acc_ref.shape, F32)

    def units(c, bias):
        k0 = pl.multiple_of(c * tk, tk)

        def scores(half):
            sl = slice(half * LANES, (half + 1) * LANES)
            s = _dot(q_ref[:, sl], k_ref[pl.ds(k0, tk), sl], _NT)
            return s if bias is None else s + bias

        return [functools.partial(scores, half) for half in range(2)], v_ref[pl.ds(k0, tk), :]

    def run(chunks):
        fns, vals = [], []
        for c, bias in chunks:
            f, v = units(c, bias)
            fns += f
            vals += [v, v]
        _flash_heads(fns, vals, m_ref, l_ref, acc_ref, states=[0, 1] * len(chunks))

    def body(cp, carry):
        run([(2 * cp, None), (2 * cp + 1, None)])
        return carry

    lax.fori_loop(0, qi // 2, body, 0)
    causal = (lax.broadcasted_iota(I32, (1, tk), 1) <= lax.broadcasted_iota(I32, (tq, 1), 0))
    diag_bias = jnp.where(causal, 0.0, NEG_INF)

    @pl.when(qi % 2 == 1)
    def _():
        run([(qi - 1, None), (qi, diag_bias)])

    @pl.when(qi % 2 == 0)
    def _():
        run([(qi, diag_bias)])


    lv = lam_ref[...]
    lam = (jnp.exp(jnp.sum(lv[0:1] * lv[1:2], axis=-1, keepdims=True))
           - jnp.exp(jnp.sum(lv[2:3] * lv[3:4], axis=-1, keepdims=True)) + lam_init)
    dv = acc_ref.shape[-1]
    o = (acc_ref[0] / _lane_tile(l_ref[0], dv) - lam * (acc_ref[1] / _lane_tile(l_ref[1], dv)))
    o = o * lax.rsqrt(jnp.mean(o * o, axis=-1, keepdims=True) + RMS_EPS)
    o_ref[...] = (o * g_ref[...] * (1.0 - lam_init)).astype(o_ref.dtype)


def diff_attention(z, lam_vec, norm_g, batch, seq, lam_init, tq=512):
    dv = 2 * LANES
    tq = tk = min(tq, seq)
    nq = seq // tq
    return pl.pallas_call(
        functools.partial(_diff_attn_kernel, tq=tq, tk=tk, lam_init=lam_init),
        grid=(batch, DIFF_HEADS, nq),
        in_specs=[pl.BlockSpec((4, LANES), lambda b, h, i: (0, 0)),
                  pl.BlockSpec((tq, dv), lambda b, h, i: (b * nq + i, h)),
                  pl.BlockSpec((seq, dv), lambda b, h, i: (b, DIFF_HEADS + h)),
                  pl.BlockSpec((seq, dv), lambda b, h, i: (b, 2 * DIFF_HEADS + h)),
                  pl.BlockSpec((1, dv), lambda b, h, i: (0, 0))],
        out_specs=pl.BlockSpec((tq, dv), lambda b, h, i: (b * nq + i, h)),
        out_shape=jax.ShapeDtypeStruct((batch * seq, DIFF_HEADS * dv), BF16),
        scratch_shapes=[pltpu.VMEM((2, tq, LANES), F32), pltpu.VMEM((2, tq, LANES), F32),
                        pltpu.VMEM((2, tq, dv), F32)],
        compiler_params=_params("parallel", "parallel", "arbitrary"),
        name="diff_attention",
    )(lam_vec, z, z, z, norm_g.reshape(1, dv))


def _hgrn_kernel(gam_ref, q_ref, f_ref, i_ref, gate_ref, ng_ref, o_ref,
                 st_ref, *, tt, layer, heads):
    c_len = HGRN_CHUNK
    width = heads * LANES

    @pl.when(pl.program_id(2) == 0)
    def _():
        st_ref[...] = jnp.zeros(st_ref.shape, F32)

    gam = gam_ref[...]
    e = jnp.exp(gam - jnp.max(gam, axis=0, keepdims=True))
    p = e / jnp.sum(e, axis=0, keepdims=True)
    lb = jnp.zeros((1, width), F32)
    for r in range(1, layer + 1):
        lb = lb + p[r:r + 1]
    log_lb = jnp.log(jnp.maximum(lb, LB_FLOOR))
    log_1m = jnp.log1p(-lb)
    trow = lax.broadcasted_iota(I32, (c_len, 1), 0)
    ng = ng_ref[...]

    def body(c, carry):
        rows = pl.ds(pl.multiple_of(c * c_len, c_len), c_len)
        z = f_ref[rows, :]
        log_sig = jnp.minimum(z, 0.0) - jnp.log1p(jnp.exp(-jnp.abs(z)))
        log_gate = log_1m + log_sig
        b_all = (jnp.maximum(log_lb, log_gate)
                 + jnp.log1p(jnp.exp(-jnp.abs(log_lb - log_gate))))
        step = 1
        while step < c_len:
            b_all = b_all + jnp.where(trow >= step, pltpu.roll(b_all, step, axis=0), 0.0)
            step *= 2
        k_all = (1.0 - lb) * jax.nn.sigmoid(-z)
        q_all = jax.nn.silu(q_ref[rows, :])
        g_all = jax.nn.silu(gate_ref[rows, :])
        for hd in range(heads):
            cols = slice(hd * LANES, (hd + 1) * LANES)
            bc, qc, kc = b_all[:, cols], q_all[:, cols], k_all[:, cols]
            ic = i_ref[rows, cols]
            st = st_ref[hd]
            o = _dot((qc * jnp.exp(bc)).astype(BF16), st.astype(BF16), _NT)
            tiles = [slice(r0, r0 + SUBLANES) for r0 in range(0, c_len, SUBLANES)]
            outs = [o[t] for t in tiles]
            for s in range(c_len):
                for n, t in enumerate(tiles):
                    if t.stop <= s:
                        continue
                    es = jnp.exp(bc[t] - bc[s:s + 1])
                    if t.start <= s:
                        es = jnp.where(trow[t] >= s, es, 0.0)
                    col = jnp.sum(qc[t] * es * kc[s:s + 1], axis=-1, keepdims=True)
                    outs[n] = outs[n] + col * ic[s:s + 1]
            o = jnp.concatenate(outs, axis=0)
            b_last = bc[c_len - 1:c_len]
            kd = kc * jnp.exp(b_last - bc)
            st_ref[hd] = st * jnp.exp(b_last) + _dot(ic.astype(BF16), kd.astype(BF16), _TN)
            y = o * lax.rsqrt(jnp.mean(o * o, axis=-1, keepdims=True) + RMS_EPS) * ng
            o_ref[rows, cols] = (y * g_all[:, cols]).astype(o_ref.dtype)
        return carry

    lax.fori_loop(0, tt // c_len, body, 0)


def hgrn2(z, gamma, norm_g, batch, seq, layer, tt=512, heads=4):
    tt = min(tt, seq)
    nt = seq // tt
    n_even = gamma.shape[0]
    ng = HGRN_HEADS // heads
    width = heads * LANES

    def col(offset):
        return pl.BlockSpec((tt, width), lambda b, h, t: (b * nt + t, offset * ng + h))

    return pl.pallas_call(
        functools.partial(_hgrn_kernel, tt=tt, layer=layer, heads=heads),
        grid=(batch, ng, nt),
        in_specs=[pl.BlockSpec((n_even, width), lambda b, h, t: (0, h)),
                  col(0), col(1), col(2), col(3),
                  pl.BlockSpec((1, LANES), lambda b, h, t: (0, 0))],
        out_specs=pl.BlockSpec((tt, width), lambda b, h, t: (b * nt + t, h)),
        out_shape=jax.ShapeDtypeStruct((batch * seq, HGRN_HEADS * LANES), BF16),
        scratch_shapes=[pltpu.VMEM((heads, LANES, LANES), F32)],
        compiler_params=_params("parallel", "parallel", "arbitrary"),
        name="hgrn2",
    )(gamma, z, z, z, z, norm_g.reshape(1, LANES))


def _topk_rows(vals, k, ids=None):
    r, n = vals.shape
    iota0 = lax.broadcasted_iota(I32, (r, n), 0) if ids is None else ids
    iota_k = lax.broadcasted_iota(I32, (k, n), 0)
    rank = jnp.full((r, n), k, I32)
    tops = []
    stacked = jnp.zeros((k, n), F32)
    for a in range(k):
        m = jnp.max(vals, axis=0, keepdims=True)
        idx = jnp.min(jnp.where(vals == m, iota0, jnp.iinfo(jnp.int32).max), axis=0, keepdims=True)
        hit = iota0 == idx
        rank = jnp.where(hit, a, rank)
        tops.append(m)
        stacked = jnp.where(iota_k == a, m, stacked)
        vals = jnp.where(hit, -jnp.inf, vals)
    return tops, stacked, rank


def _compress_kernel(x_ref, w_ref, pe_ref, o_ref):
    pq = _dot(x_ref[...], w_ref[...])
    nrow = pq.shape[0]
    pe_c = _dot(pe_ref[...], w_ref[...])
    const = pe_c[0:1, :LANES] + pe_c[1:2, LANES:]
    nxt = pltpu.roll(pq[:, LANES:], nrow - 1, axis=0)
    o_ref[...] = (pq[:, :LANES] + nxt + const).astype(o_ref.dtype)


def nsa_compress(xc, wcat, pecat):
    bsz, nj, nrow, kk = xc.shape
    return pl.pallas_call(
        _compress_kernel,
        grid=(bsz, nj),
        in_specs=[pl.BlockSpec((None, None, nrow, kk), lambda b, j: (b, j, 0, 0)),
                  pl.BlockSpec((None, kk, 2 * LANES), lambda b, j: (j // NSA_GROUPS, 0, 0)),
                  pl.BlockSpec((None, 8, kk), lambda b, j: (j // NSA_GROUPS, 0, 0))],
        out_specs=pl.BlockSpec((None, None, nrow, LANES), lambda b, j: (b, j, 0, 0)),
        out_shape=jax.ShapeDtypeStruct((bsz, nj, nrow, LANES), BF16),
        compiler_params=_params("parallel", "parallel"),
        name="nsa_compress",
    )(xc, wcat, pecat)


def _nsa_kernel(q_ref, kc_ref, vc_ref, ks_ref, vs_ref, kw_ref, vw_ref, gt_ref, o_ref,
                m_ref, l_ref, acc_ref, mw_ref, lw_ref, accw_ref, mc_ref, lc_ref, accc_ref,
                psum_ref, sel_ref, *, tq, tk, ncp, nsp, n_top):
    rep = NSA_REP
    q0 = pl.program_id(2) * tq
    qpos = q0 + lax.broadcasted_iota(I32, (tq, 1), 0)

    def q_head(r):
        return q_ref[:, r * LANES:(r + 1) * LANES]

    def once(fn):
        lax.fori_loop(0, jnp.minimum(pl.program_id(2), 0) + 1, lambda _, carry: (fn(), carry)[1], 0)

    has_key = qpos >= CMP_LEN - 1

    def compressed():
        for ref in (m_ref, mw_ref, mc_ref):
            ref[...] = jnp.full(ref.shape, NEG_INF, F32)
        for ref in (l_ref, acc_ref, lw_ref, accw_ref, lc_ref, accc_ref):
            ref[...] = jnp.zeros(ref.shape, F32)
        cend = lax.broadcasted_iota(I32, (1, ncp), 1) * CMP_STRIDE + (CMP_LEN - 1)
        bias_c = jnp.where(cend <= qpos, 0.0, NEG_INF)
        psum_ref[...] = jnp.zeros(psum_ref.shape, F32)
        kc = kc_ref[...]
        _flash_heads([functools.partial(lambda r: _dot(q_head(r), kc, _NT) + bias_c, r)
                      for r in range(rep)], vc_ref[...], mc_ref, lc_ref, accc_ref, psum_ref, has_key)

    once(compressed)

    span = WINDOW + tq
    start = pl.multiple_of(jnp.maximum(q0 - WINDOW, 0), tq)

    def window_and_selection():
        n_i = lax.broadcasted_iota(I32, (ncp, nsp), 0)
        j4 = lax.broadcasted_iota(I32, (ncp, nsp), 1) * (SLC_LEN // CMP_STRIDE)
        overlap = ((n_i >= j4 - (CMP_LEN - 1) // CMP_STRIDE)
                   & (n_i < j4 + SLC_LEN // CMP_STRIDE)).astype(F32)
        imp_t = _dot(psum_ref[...], overlap, precision=lax.Precision.HIGHEST).T
        jt = lax.broadcasted_iota(I32, (nsp, 1), 0)
        cur = (q0 + lax.broadcasted_iota(I32, (1, tq), 1)) // SLC_LEN
        forced = (jt == 0) | (jt == cur) | (jt == cur - 1)
        imp_t = jnp.where(forced, FORCE_SCORE, jnp.where(jt <= cur, imp_t, -1.0))
        _, _, rank = _topk_rows(imp_t, n_top)
        sel_ref[...] = (rank < n_top).astype(F32).T.astype(BF16)

        kpos = start + lax.broadcasted_iota(I32, (1, span), 1)
        bias_w = jnp.where((kpos <= qpos) & (kpos > qpos - WINDOW), 0.0, NEG_INF)
        k_w = kw_ref[pl.ds(start, span), :]
        _flash_heads([functools.partial(lambda r: _dot(q_head(r), k_w, _NT) + bias_w, r)
                      for r in range(rep)], vw_ref[pl.ds(start, span), :], mw_ref, lw_ref, accw_ref)

    once(window_and_selection)

    jj = lax.broadcasted_iota(I32, (nsp, 1), 0)

    def body(c, carry):
        k0 = pl.multiple_of(c * tk, tk)
        k = ks_ref[pl.ds(k0, tk), :]
        v = vs_ref[pl.ds(k0, tk), :]
        kpos = k0 + lax.broadcasted_iota(I32, (1, tk), 1)
        expand = (jj == kpos // SLC_LEN).astype(BF16)
        allowed = (_dot(sel_ref[...], expand) > 0.5) & (kpos <= qpos)
        bias = jnp.where(allowed, 0.0, NEG_INF)
        _flash_heads([functools.partial(lambda r: _dot(q_head(r), k, _NT) + bias, r)
                      for r in range(rep)], v, m_ref, l_ref, acc_ref)
        return carry

    lax.fori_loop(0, (q0 + tq + tk - 1) // tk, body, 0)

    gt = gt_ref[...]
    for r in range(rep):
        o = (gt[:, 3 * r:3 * r + 1] * jnp.where(has_key, accc_ref[r] / lc_ref[r], 0.0)
             + gt[:, 3 * r + 1:3 * r + 2] * (acc_ref[r] / l_ref[r])
             + gt[:, 3 * r + 2:3 * r + 3] * (accw_ref[r] / lw_ref[r]))
        o_ref[:, r * LANES:(r + 1) * LANES] = o.astype(o_ref.dtype)


def nsa_attention(z, kvc, gates, batch, seq, tq=128, tk=1024):
    tk = min(tk, seq)
    assert seq % tq == 0 and seq >= WINDOW + tq and tq % SLC_LEN == 0 and seq % tk == 0
    nq = seq // tq
    ncp = seq // CMP_STRIDE
    n_slc = seq // SLC_LEN
    nsp = -(-n_slc // LANES) * LANES
    n_top = min(SLC_TOPN, n_slc)
    qw = NSA_REP * LANES
    g = NSA_GROUPS
    q_cols = g * NSA_REP

    def kv_col(which):
        return pl.BlockSpec((seq, LANES), lambda b, gi, i: (b, q_cols + which * g + gi))

    head_state = pltpu.VMEM((NSA_REP, tq, LANES), F32)
    return pl.pallas_call(
        functools.partial(_nsa_kernel, tq=tq, tk=tk, ncp=ncp, nsp=nsp, n_top=n_top),
        grid=(batch, g, nq),
        in_specs=[pl.BlockSpec((tq, qw), lambda b, gi, i: (b * nq + i, gi)),
                  pl.BlockSpec((None, None, ncp, LANES), lambda b, gi, i: (b, gi, 0, 0)),
                  pl.BlockSpec((None, None, ncp, LANES), lambda b, gi, i: (b, g + gi, 0, 0)),
                  kv_col(2), kv_col(3), kv_col(4), kv_col(5),
                  pl.BlockSpec((tq, LANES), lambda b, gi, i: (b * nq + i, gi))],
        out_specs=pl.BlockSpec((tq, qw), lambda b, gi, i: (b * nq + i, gi)),
        out_shape=jax.ShapeDtypeStruct((batch * seq, g * qw), BF16),
        scratch_shapes=[head_state] * 9 + [pltpu.VMEM((tq, ncp), F32), pltpu.VMEM((tq, nsp), BF16)],
        compiler_params=_params("parallel", "parallel", "arbitrary"),
        name="nsa_attention",
    )(z, kvc, kvc, z, z, z, z, gates)


def _xattn_kernel(hb_ref, h_ref, wq_ref, kv_ref, wo_ref, g_ref, b_ref, o_ref, ob_ref, *, alpha):
    width = XATTN_HEADS * LANES
    scale = LANES ** -0.5
    q = _dot(hb_ref[...], wq_ref[...]).astype(BF16)
    kv = kv_ref[...]
    outs = []
    for hd in range(XATTN_HEADS):
        sl = slice(hd * LANES, (hd + 1) * LANES)
        s = _dot(q[:, sl], kv[:, sl], _NT) * scale
        e = jnp.exp(s - jnp.max(s, axis=-1, keepdims=True))
        p = e / jnp.sum(e, axis=-1, keepdims=True)
        outs.append(_dot(p.astype(BF16), kv[:, width + hd * LANES:width + (hd + 1) * LANES]))
    o = jnp.concatenate(outs, axis=1).astype(BF16)
    y = _dot(o, wo_ref[...])
    out = _layer_norm_rows(alpha * h_ref[...] + y, g_ref[...], b_ref[...])
    o_ref[...] = out
    ob_ref[...] = out.astype(BF16)


def xattn_block(hb, h, wq, kv, wo, g, b, alpha, batch, seq, tm=256):
    n, d = h.shape
    tm = min(tm, seq)
    nt = seq // tm
    mem_len = kv.shape[0] // batch
    width = XATTN_HEADS * LANES
    return pl.pallas_call(
        functools.partial(_xattn_kernel, alpha=alpha),
        grid=(batch, nt),
        in_specs=[pl.BlockSpec((tm, d), lambda bi, i: (bi * nt + i, 0)),
                  pl.BlockSpec((tm, d), lambda bi, i: (bi * nt + i, 0)),
                  pl.BlockSpec((d, width), lambda bi, i: (0, 0)),
                  pl.BlockSpec((mem_len, 2 * width), lambda bi, i: (bi, 0)),
                  pl.BlockSpec((width, d), lambda bi, i: (0, 0)),
                  pl.BlockSpec((1, d), lambda bi, i: (0, 0)),
                  pl.BlockSpec((1, d), lambda bi, i: (0, 0))],
        out_specs=[pl.BlockSpec((tm, d), lambda bi, i: (bi * nt + i, 0)),
                   pl.BlockSpec((tm, d), lambda bi, i: (bi * nt + i, 0))],
        out_shape=[jax.ShapeDtypeStruct((n, d), F32), jax.ShapeDtypeStruct((n, d), BF16)],
        compiler_params=_params("parallel", "parallel"),
        name="xattn_block",
    )(hb, h, wq, kv, wo, g.reshape(1, d), b.reshape(1, d))


def _peer_route_kernel(q_ref, keys_ref, mb_ref, e1_ref, b2_ref, e2_ref, *, tn):
    k = PEER_TOPK
    nk = PEER_N_KEYS
    hi = lax.Precision.HIGHEST
    for h in range(PEER_HEADS):
        s1 = _dot(keys_ref[h, 0], q_ref[:, (2 * h) * LANES:(2 * h + 1) * LANES], _NT, hi)
        s2 = _dot(keys_ref[h, 1], q_ref[:, (2 * h + 1) * LANES:(2 * h + 2) * LANES], _NT, hi)
        v1, v1_all, rank1 = _topk_rows(s1, k)
        v2, v2_all, rank2 = _topk_rows(s2, k)
        half = k // 2
        iota_a = lax.broadcasted_iota(I32, (k, tn), 0)
        is_hi = iota_a >= half
        cand = [v1_all + v2[b] for b in range(half)] + [jnp.where(is_hi, v1[0] + v2_all, -jnp.inf)]
        ids = [iota_a * k + b for b in range(half)] + [jnp.where(is_hi, iota_a, iota_a + k * k)]
        tops, _, crank = _topk_rows(jnp.concatenate(cand, axis=0), k, jnp.concatenate(ids, axis=0))
        zsum = tops[0] * 0.0
        for a in range(k):
            zsum = zsum + jnp.exp(tops[a] - tops[0])
        cnt = jnp.zeros((k, tn), F32)
        for b in range(half):
            cnt = cnt + jnp.where(crank[b * k:(b + 1) * k] < k, 1.0, 0.0)
        won_hi = (crank[half * k:(half + 1) * k] < k) & is_hi
        extra = jnp.sum(jnp.where(won_hi, 1.0, 0.0), axis=0, keepdims=True)
        cnt = (cnt + jnp.where(iota_a == 0, extra, 0.0)).astype(I32)
        mb = jnp.zeros((nk, tn), I32)
        for a in range(k):
            mb = jnp.where(rank1 == a, cnt[a:a + 1], mb)
        mb_ref[h] = mb
        b2_ref[h] = rank2.astype(b2_ref.dtype)
        e1_ref[h] = jnp.where(rank1 < k, jnp.exp(s1 - v1[0]), 0.0) / zsum
        e2_ref[h] = jnp.where(rank2 < k, jnp.exp(s2 - v2[0]), 0.0).astype(e2_ref.dtype)


def peer_route(q, keys, tn=256):
    n = q.shape[0]
    tn = min(tn, n)
    tab = lambda dt: jax.ShapeDtypeStruct((PEER_HEADS, PEER_N_KEYS, n), dt)
    tspec = pl.BlockSpec((PEER_HEADS, PEER_N_KEYS, tn), lambda i: (0, 0, i))
    return pl.pallas_call(
        functools.partial(_peer_route_kernel, tn=tn),
        grid=(n // tn,),
        in_specs=[pl.BlockSpec((tn, q.shape[1]), lambda i: (i, 0)),
                  pl.BlockSpec(keys.shape, lambda i: (0, 0, 0, 0))],
        out_specs=[tspec, tspec, tspec, tspec],
        out_shape=[tab(I32), tab(F32), tab(jnp.int16), tab(BF16)],
        compiler_params=_params("parallel"),
        name="peer_route",
    )(q, keys)


def _peer_dense_kernel(x_ref, u_ref, vt_ref, mb_ref, e1_ref, b2_ref, e2_ref, o_ref, w_ref,
                       *, te, tn):
    j = pl.program_id(1)

    @pl.when(j == 0)
    def _():
        o_ref[...] = jnp.zeros(o_ref.shape, F32)

    nk = PEER_N_KEYS
    for sub in range(te // nk):
        i1 = j * (te // nk) + sub
        w = jnp.zeros((nk, tn), BF16)
        for h in range(PEER_HEADS):
            hit = b2_ref[h] < mb_ref[h, pl.ds(i1, 1), :].astype(jnp.int16)
            gate = e1_ref[h, pl.ds(i1, 1), :].astype(BF16) * e2_ref[h]
            w = w + jnp.where(hit, gate, jnp.zeros((), BF16))
        w_ref[sub * nk:(sub + 1) * nk, :] = w

    chains = list(range(0, te, PEER_CHAIN))
    hts = [_dot(u_ref[c0:c0 + PEER_CHAIN, :], x_ref[...], _NT) for c0 in chains]
    for c0, ht in zip(chains, hts):
        gelu = 0.5 * ht * (1.0 + lax.erf(ht * (2.0 ** -0.5)))
        a = gelu.astype(BF16) * w_ref[c0:c0 + PEER_CHAIN, :]
        o_ref[...] += _dot(vt_ref[:, c0:c0 + PEER_CHAIN], a)


def peer_dense(xb, u, vt, tables, tn=512, te=512):
    n, d = xb.shape
    n_exp = u.shape[0]
    tn = min(tn, n)
    assert te % PEER_CHAIN == 0 and n_exp % te == 0
    once = pl.Buffered(1)
    tspec = pl.BlockSpec((PEER_HEADS, PEER_N_KEYS, tn), lambda i, j: (0, 0, i), pipeline_mode=once)
    return pl.pallas_call(
        functools.partial(_peer_dense_kernel, te=te, tn=tn),
        grid=(n // tn, n_exp // te),
        in_specs=[pl.BlockSpec((tn, d), lambda i, j: (i, 0), pipeline_mode=once),
                  pl.BlockSpec((te, d), lambda i, j: (j, 0)),
                  pl.BlockSpec((d, te), lambda i, j: (0, j)),
                  tspec, tspec, tspec, tspec],
        out_specs=pl.BlockSpec((d, tn), lambda i, j: (0, i)),
        out_shape=jax.ShapeDtypeStruct((d, n), F32),
        scratch_shapes=[pltpu.VMEM((te, tn), BF16)],
        compiler_params=_params("parallel", "arbitrary"),
        name="peer_dense",
    )(xb, u, vt, *tables)


def odd_mixer_core(hb, w_in, cmp_pe, cmp_w, batch, seq, nsa_tq=128):
    q_w = NSA_GROUPS * NSA_REP * LANES
    kv_w = NSA_GROUPS * LANES
    nsa_main = q_w + 6 * kv_w
    n_gate = 3 * NSA_GROUPS * NSA_REP
    z = matmul(hb, w_in[:, :nsa_main].astype(BF16), BF16, scaled_cols=q_w, scale=QK_LOG2_SCALE)
    per_group = n_gate // NSA_GROUPS
    w_gate = w_in[:, nsa_main:].reshape(-1, NSA_GROUPS, per_group)
    w_gate = jnp.pad(w_gate, ((0, 0), (0, 0), (0, LANES - per_group)))
    w_gate = w_gate.reshape(-1, NSA_GROUPS * LANES).astype(BF16)
    gates = matmul(hb, w_gate, F32, act="sigmoid")
    xc = z[:, q_w:q_w + 2 * kv_w].reshape(batch, seq // CMP_STRIDE, CMP_STRIDE, 2 * NSA_GROUPS, LANES)
    xc = xc.transpose(0, 3, 1, 2, 4).reshape(
        batch, 2 * NSA_GROUPS, seq // CMP_STRIDE, CMP_STRIDE * LANES)
    half = CMP_STRIDE * LANES
    cw = cmp_w.astype(BF16)
    wcat = jnp.concatenate([cw[:, :half], cw[:, half:]], axis=2)
    pe = cmp_pe.reshape(2, 2, 1, half).astype(BF16)
    pecat = jnp.concatenate([pe[:, 0], pe[:, 1], jnp.zeros((2, 6, half), BF16)], axis=1)
    kvc = nsa_compress(xc, wcat, pecat)
    return nsa_attention(z, kvc, gates, batch, seq, tq=nsa_tq)


def even_mixer_core(hb, w_in, lam_vec, diff_g, gamma, hgrn_g, batch, seq, lam_init, layer):
    diff_w = DIFF_HEADS * 2 * LANES
    attn_cols = 3 * diff_w
    w_in = w_in.astype(BF16)
    z_attn = matmul(hb, w_in[:, :attn_cols], BF16, scaled_cols=diff_w, scale=QK_LOG2_SCALE)
    z_rec = matmul(hb, w_in[:, attn_cols:], F32)
    o_a = diff_attention(z_attn, lam_vec, diff_g, batch, seq, lam_init)
    o_b = hgrn2(z_rec, gamma, hgrn_g, batch, seq, layer)
    return o_a, o_b


def peer_core(hb, wq, keys, u, v):
    pq = matmul(hb, wq.astype(BF16), F32)
    tables = peer_route(pq, keys)
    return peer_dense(hb, u.astype(BF16), v.T.astype(BF16), tables)


def kernel(x, mem, w_in_even, w_out_even, diff_lambda, diff_norm_g, hgrn_gamma, hgrn_norm_g,
           w_in_odd, w_out_odd, cmp_pe, cmp_w, xattn_wq, xattn_wkv, xattn_wo, ln_g, ln_b,
           peer_wq, peer_keys, peer_u, peer_v):
    batch, seq, d = x.shape
    depth = ln_g.shape[0]
    n = batch * seq
    assert d == 32 * LANES, "head dims are d_model/32 and must equal the lane count"
    alpha = (2.0 * depth) ** 0.25

    h = x.reshape(n, d)
    hb = h.astype(BF16)
    memb = mem.reshape(-1, d).astype(BF16)

    for l in range(depth):
        j = l // 2
        if l % 2 == 0:
            lam_init = 0.8 - 0.6 * math.exp(-0.3 * l)
            o_a, o_b = even_mixer_core(hb, w_in_even[j], diff_lambda[j], diff_norm_g[j], hgrn_gamma,
                                       hgrn_norm_g[j], batch, seq, lam_init, j)
            y = matmul_pair(o_a, o_b, w_out_even[j].astype(BF16))
        else:
            o = odd_mixer_core(hb, w_in_odd[j], cmp_pe[j], cmp_w[j], batch, seq)
            y = matmul(o, w_out_odd[j].astype(BF16), F32)
        h, hb = ln_residual(h, y, ln_g[l, 0], ln_b[l, 0], alpha)

        kv = matmul(memb, xattn_wkv[l].astype(BF16), BF16)
        h, hb = xattn_block(hb, h, xattn_wq[l].astype(BF16), kv, xattn_wo[l].astype(BF16),
                            ln_g[l, 1], ln_b[l, 1], alpha, batch, seq)

        yt = peer_core(hb, peer_wq[l], peer_keys[l], peer_u[l], peer_v[l])
        h, hb = ln_residual(h, yt, ln_g[l, 2], ln_b[l, 2], alpha, y_transposed=True)

    return h.reshape(batch, seq, d)
```
